```python
import jax, jax.numpy as jnp
from jax import lax
import numpy as np

D_MODEL = 1024
BATCH = 8
SEQ = 2048
DEPTH = 4
DEC_BATCH = 128
DEC_SEQ = 8
PAST_LEN = 16384
PAGE_SIZE = 128

MIX_WIDTH = D_MODEL // 2
HEAD_SIZE = 64
N_HEADS = MIX_WIDTH // HEAD_SIZE
D_DECAY_LORA = 64
D_AAA_LORA = 64
D_MV_LORA = 32
D_GATE_LORA = 128
POOL_WIDTH = D_MODEL // 2
POOL_WINDOWS = (2, 4, 8, 16)
N_POOL_GROUPS = len(POOL_WINDOWS)
POOL_GROUP = POOL_WIDTH // N_POOL_GROUPS
POOL_BUF = max(POOL_WINDOWS) - 1
D_FF = 4 * D_MODEL
N_BRANCHES = 2
RWKV_COLS = 3 * MIX_WIDTH + D_DECAY_LORA + D_AAA_LORA + D_GATE_LORA
IN_COLS = RWKV_COLS + POOL_WIDTH + N_BRANCHES * D_MODEL
RWKV_SPLITS = (MIX_WIDTH, 2 * MIX_WIDTH, 3 * MIX_WIDTH,
               3 * MIX_WIDTH + D_DECAY_LORA, 3 * MIX_WIDTH + D_DECAY_LORA + D_AAA_LORA)
NORM_EPS = 1e-6
GN_EPS = 64e-5

kernel_name = "rwkv7_pool_gated_hybrid_step"

V_NAMES = ("v0", "v1", "v2")


def rms_norm(x, g):
    xf = x.astype(jnp.float32)
    y = xf * lax.rsqrt(jnp.mean(xf * xf, axis=-1, keepdims=True) + NORM_EPS)
    return (y * g.astype(jnp.float32)).astype(x.dtype)


def ada_mod(c, w, b):
    m = jax.nn.silu(c) @ w + b
    shift, scale, gate = jnp.split(m, 3, axis=-1)
    return shift[:, None], scale[:, None], gate[:, None]


def wkv7_scan(state0, r, decay, k, v, kk, a):
    def step(S, inp):
        r_t, w_t, k_t, v_t, kk_t, a_t = inp
        sa = jnp.einsum("bhvk,bhk->bhv", S, -kk_t)
        S = (S * w_t[:, :, None, :]
             + sa[..., None] * (kk_t * a_t)[:, :, None, :]
             + v_t[..., None] * k_t[:, :, None, :])
        y = jnp.einsum("bhvk,bhk->bhv", S, r_t)
        return S, y
    xs = tuple(jnp.moveaxis(t, 1, 0) for t in (r, decay, k, v, kk, a))
    S, ys = lax.scan(step, state0, xs)
    return jnp.moveaxis(ys, 0, 1), S


def rwkv7_branch(pr, shift_prev, wkv_prev, v_first, lp):
    B, T, _ = pr.shape
    f32 = jnp.float32
    pr_prev = jnp.concatenate([shift_prev[:, None, :].astype(pr.dtype), pr[:, :-1]], axis=1)
    xl = pr + (pr_prev - pr) * lp["mu_shift"]
    r, k, v, xw, xa, xg = jnp.split(xl, RWKV_SPLITS, axis=-1)
    w_logit = (lp["w0"] + jnp.tanh(xw) @ lp["w2"]).astype(f32)
    decay = jnp.exp(-jnp.exp(-jax.nn.softplus(-w_logit) - 0.5))
    a = jax.nn.sigmoid(lp["a0"] + xa @ lp["a2"])
    g = jax.nn.sigmoid(xg) @ lp["g2"]
    if v_first is None:
        v_first = v
    else:
        v = v + (v_first - v) * jax.nn.sigmoid(lp["v0"] + (v @ lp["v1"]) @ lp["v2"])
    heads = lambda t: t.reshape(B, T, N_HEADS, HEAD_SIZE).astype(f32)
    kk = heads(k * lp["k_k"])
    kk = kk * lax.rsqrt(jnp.sum(kk * kk, axis=-1, keepdims=True) + 1e-12)
    k = k * (1 + (a - 1) * lp["k_a"])
    rh, kh, vh, ah = heads(r), heads(k), heads(v), heads(a)
    y, wkv_new = wkv7_scan(wkv_prev.astype(f32), rh, heads(decay), kh, vh, kk, ah)
    mean = jnp.mean(y, axis=-1, keepdims=True)
    var = jnp.mean(jnp.square(y - mean), axis=-1, keepdims=True)
    y = ((y - mean) * lax.rsqrt(var + GN_EPS)).reshape(B, T, MIX_WIDTH)
    y = y * lp["ln_w"].astype(f32) + lp["ln_b"].astype(f32)
    bonus = jnp.sum(rh * kh * lp["r_k"].astype(f32), axis=-1, keepdims=True) * vh
    y = (y + bonus.reshape(B, T, MIX_WIDTH)).astype(pr.dtype) * g
    return y, pr[:, -1], wkv_new, v_first


def pool_branch(pp, pool_prev, pos0, lp):
    B, T, _ = pp.shape
    buf = jnp.concatenate([pool_prev.astype(pp.dtype), pp], axis=1)
    cs = jnp.pad(jnp.cumsum(buf.astype(jnp.float32), axis=1), ((0, 0), (1, 0), (0, 0)))
    pos = pos0 + jnp.arange(T, dtype=jnp.int32)
    hi = cs[:, POOL_BUF + 1:]
    means = []
    for gi, win in enumerate(POOL_WINDOWS):
        ch = slice(gi * POOL_GROUP, (gi + 1) * POOL_GROUP)
        lo = cs[:, POOL_BUF + 1 - win: POOL_BUF + 1 - win + T, ch]
        cnt = jnp.minimum(pos + 1, win).astype(jnp.float32)[None, :, None]
        means.append((hi[..., ch] - lo) / cnt)
    pooled = jnp.concatenate(means, axis=-1).astype(pp.dtype) - pp
    z = jnp.einsum("btgc,gcd->btgd", pooled.reshape(B, T, N_POOL_GROUPS, POOL_GROUP), lp["pool_w"])
    return z.reshape(B, T, POOL_WIDTH) * lp["pool_scale"], buf[:, -POOL_BUF:]


def trunk(x, c, shift0, pool0, wkv0, pos0, W):
    shifts, pools, wkvs = [], [], []
    v_first = None
    for l in range(DEPTH):
        lp = {name: arr[l] for name, arr in W.items() if name not in V_NAMES}
        if l > 0:
            lp.update({name: W[name][l - 1] for name in V_NAMES})
        sh, sc, gt = ada_mod(c, lp["w_ada_mix"], lp["b_ada_mix"])
        h = rms_norm(x, lp["norm_mix"]) * (1 + sc) + sh
        P = h @ lp["w_in"]
        pr, pp, pg = jnp.split(P, [RWKV_COLS, RWKV_COLS + POOL_WIDTH], axis=-1)
        o_r, s_shift, s_wkv, v_first = rwkv7_branch(pr, shift0[l], wkv0[l], v_first, lp)
        o_p, s_pool = pool_branch(pp, pool0[l], pos0, lp)
        g_r, g_p = jnp.split(jax.nn.sigmoid(pg), 2, axis=-1)
        merged = g_r * (o_r @ lp["w_br_rwkv"]) + g_p * (o_p @ lp["w_br_pool"])
        x = x + gt * (merged @ lp["w_out"])
        sh, sc, gt = ada_mod(c, lp["w_ada_mlp"], lp["b_ada_mlp"])
        h = rms_norm(x, lp["norm_mlp"]) * (1 + sc) + sh
        x = x + gt * (jnp.square(jax.nn.relu(h @ lp["w_ff1"])) @ lp["w_ff2"])
        shifts.append(s_shift)
        pools.append(s_pool)
        wkvs.append(s_wkv)
    y = rms_norm(x, W["norm_final"])
    return y, jnp.stack(shifts), jnp.stack(pools), jnp.stack(wkvs)


def setup_inputs(seed: int = 0) -> dict:
    key = jax.random.key(seed)
    ks = iter(jax.random.split(key, 48))
    nrm = lambda shape, s: jax.random.normal(next(ks), shape, jnp.float32) * s
    uni = lambda shape, lo, hi: jax.random.uniform(next(ks), shape, jnp.float32, lo, hi)
    L, D, M = DEPTH, D_MODEL, MIX_WIDTH
    return {
        "x_prompt": nrm((BATCH, SEQ, D), 1.0),
        "x_sample": nrm((DEC_BATCH, DEC_SEQ, D), 1.0),
        "state_shift": nrm((L, DEC_BATCH, RWKV_COLS), 1.0),
        "state_pool": nrm((L, DEC_BATCH, POOL_BUF, POOL_WIDTH), 1.0),
        "state_wkv": nrm((L, DEC_BATCH, N_HEADS, HEAD_SIZE, HEAD_SIZE), 0.3),
        "c_prompt": nrm((BATCH, D), 1.0),
        "c_sample": nrm((DEC_BATCH, D), 1.0),
        "w_ada_mix": nrm((L, D, 3 * D), 0.5 * D ** -0.5),
        "b_ada_mix": nrm((L, 3 * D), 0.02),
        "norm_mix": 1.0 + nrm((L, D), 0.1),
        "w_in": nrm((L, D, IN_COLS), D ** -0.5),
        "mu_shift": uni((L, RWKV_COLS), 0.0, 1.0),
        "w0": uni((L, M), -6.0, 0.0),
        "w2": nrm((L, D_DECAY_LORA, M), 0.1),
        "a0": nrm((L, M), 0.1),
        "a2": nrm((L, D_AAA_LORA, M), D_AAA_LORA ** -0.5),
        "g2": nrm((L, D_GATE_LORA, M), D_GATE_LORA ** -0.5),
        "v0": nrm((L - 1, M), 0.1),
        "v1": nrm((L - 1, M, D_MV_LORA), M ** -0.5),
        "v2": nrm((L - 1, D_MV_LORA, M), D_MV_LORA ** -0.5),
        "k_k": 0.85 + nrm((L, M), 0.05),
        "k_a": 1.0 + nrm((L, M), 0.1),
        "r_k": nrm((L, N_HEADS, HEAD_SIZE), 0.1),
        "ln_w": 1.0 + nrm((L, M), 0.1),
        "ln_b": nrm((L, M), 0.01),
        "pool_w": nrm((L, N_POOL_GROUPS, POOL_GROUP, POOL_GROUP), POOL_GROUP ** -0.5),
        "pool_scale": 1.0 + nrm((L, POOL_WIDTH), 0.1),
        "w_br_rwkv": nrm((L, M, D), M ** -0.5),
        "w_br_pool": nrm((L, POOL_WIDTH, D), POOL_WIDTH ** -0.5),
        "w_out": nrm((L, D, D), D ** -0.5),
        "w_ada_mlp": nrm((L, D, 3 * D), 0.5 * D ** -0.5),
        "b_ada_mlp": nrm((L, 3 * D), 0.02),
        "norm_mlp": 1.0 + nrm((L, D), 0.1),
        "w_ff1": nrm((L, D, D_FF), D ** -0.5),
        "w_ff2": nrm((L, D_FF, D), D_FF ** -0.5),
        "norm_final": 1.0 + nrm((D,), 0.1),
    }


def reference(x_prompt, x_sample, state_shift, state_pool, state_wkv, c_prompt, c_sample,
              w_ada_mix, b_ada_mix, norm_mix, w_in, mu_shift, w0, w2, a0, a2, g2, v0, v1, v2,
              k_k, k_a, r_k, ln_w, ln_b, pool_w, pool_scale, w_br_rwkv, w_br_pool, w_out,
              w_ada_mlp, b_ada_mlp, norm_mlp, w_ff1, w_ff2, norm_final):
    W = {
        "w_ada_mix": w_ada_mix, "b_ada_mix": b_ada_mix, "norm_mix": norm_mix, "w_in": w_in,
        "mu_shift": mu_shift, "w0": w0, "w2": w2, "a0": a0, "a2": a2, "g2": g2,
        "v0": v0, "v1": v1, "v2": v2, "k_k": k_k, "k_a": k_a, "r_k": r_k,
        "ln_w": ln_w, "ln_b": ln_b, "pool_w": pool_w, "pool_scale": pool_scale,
        "w_br_rwkv": w_br_rwkv, "w_br_pool": w_br_pool, "w_out": w_out,
        "w_ada_mlp": w_ada_mlp, "b_ada_mlp": b_ada_mlp, "norm_mlp": norm_mlp,
        "w_ff1": w_ff1, "w_ff2": w_ff2, "norm_final": norm_final,
    }
    Bp = x_prompt.shape[0]
    shift0 = jnp.zeros((DEPTH, Bp, RWKV_COLS), x_prompt.dtype)
    pool0 = jnp.zeros((DEPTH, Bp, POOL_BUF, POOL_WIDTH), x_prompt.dtype)
    wkv0 = jnp.zeros((DEPTH, Bp, N_HEADS, HEAD_SIZE, HEAD_SIZE), jnp.float32)
    y_prompt, shift_p, pool_p, wkv_p = trunk(x_prompt, c_prompt, shift0, pool0, wkv0, 0, W)
    y_sample, shift_s, pool_s, wkv_s = trunk(x_sample, c_sample, state_shift, state_pool,
                                             state_wkv, PAST_LEN, W)
    return (y_prompt, y_sample, shift_p, pool_p, wkv_p, shift_s, pool_s, wkv_s)
```

```python
import functools

import jax
import jax.numpy as jnp
from jax import lax
from jax.experimental import pallas as pl
from jax.experimental.pallas import tpu as pltpu

D_MODEL = 1024
DEPTH = 4
PAST_LEN = 16384
MIX_WIDTH = D_MODEL // 2
HEAD_SIZE = 64
N_HEADS = MIX_WIDTH // HEAD_SIZE
D_DECAY_LORA = 64
D_AAA_LORA = 64
D_GATE_LORA = 128
POOL_WIDTH = D_MODEL // 2
POOL_WINDOWS = (2, 4, 8, 16)
N_POOL_GROUPS = len(POOL_WINDOWS)
POOL_GROUP = POOL_WIDTH // N_POOL_GROUPS
POOL_BUF = max(POOL_WINDOWS) - 1
D_FF = 4 * D_MODEL
RWKV_COLS = 3 * MIX_WIDTH + D_DECAY_LORA + D_AAA_LORA + D_GATE_LORA
GATE_COLS = 2 * D_MODEL
IN_COLS = RWKV_COLS + POOL_WIDTH + GATE_COLS
RWKV_SPLITS = (MIX_WIDTH, 2 * MIX_WIDTH, 3 * MIX_WIDTH,
               3 * MIX_WIDTH + D_DECAY_LORA, 3 * MIX_WIDTH + D_DECAY_LORA + D_AAA_LORA)
NORM_EPS = 1e-6
GN_EPS = 64e-5

V7X_VMEM_LIMIT_BYTES = 56 * 1024 * 1024
ROW_TILE = 256
FF_CHUNK = 1024
WKV_TIME_CHUNK = 128

_BF16 = jnp.bfloat16
_F32 = jnp.float32


def _cparams(*sem):
    return pltpu.CompilerParams(dimension_semantics=sem, vmem_limit_bytes=V7X_VMEM_LIMIT_BYTES)


def _dot(a, b):
    return jnp.dot(a, b, preferred_element_type=_F32)


def _rms(x, g):
    return x * lax.rsqrt(jnp.mean(x * x, axis=-1, keepdims=True) + NORM_EPS) * g


def _ada_kernel(c_ref, w_ref, b_ref, o_ref):
    c = c_ref[...]
    s = (c * jax.nn.sigmoid(c)).astype(_BF16)
    o_ref[0] = _dot(s, w_ref[0].astype(_BF16)) + b_ref[0]


def _ada_mod(c, w, b, tn=1024):
    L, D, N = w.shape
    M = c.shape[0]
    return pl.pallas_call(
        _ada_kernel,
        grid=(L, N // tn),
        in_specs=[
            pl.BlockSpec((M, D), lambda l, j: (0, 0)),
            pl.BlockSpec((1, D, tn), lambda l, j: (l, 0, j)),
            pl.BlockSpec((1, 1, tn), lambda l, j: (l, 0, j)),
        ],
        out_specs=pl.BlockSpec((1, M, tn), lambda l, j: (l, 0, j)),
        out_shape=jax.ShapeDtypeStruct((L, M, N), _F32),
        compiler_params=_cparams("arbitrary", "arbitrary"),
        name="ada_mod",
    )(c, w, b.reshape(L, 1, N))


def _in_kernel(x_ref, sc_ref, sh_ref, g_ref, w_ref, pr_ref, pp_ref, pg_ref):
    bt, tt, d = x_ref.shape
    h = _rms(x_ref[...], g_ref[...]) * (1.0 + sc_ref[...]) + sh_ref[...]
    hb = h.reshape(bt * tt, d).astype(_BF16)
    pr_ref[...] = _dot(hb, w_ref[:, :RWKV_COLS]).reshape(bt, tt, RWKV_COLS)
    pp_ref[...] = _dot(hb, w_ref[:, RWKV_COLS:RWKV_COLS + POOL_WIDTH]).reshape(bt, tt, POOL_WIDTH)
    pg_ref[...] = _dot(hb, w_ref[:, RWKV_COLS + POOL_WIDTH:]).reshape(bt, tt, GATE_COLS)


def _tile(B, T):
    tt = min(T, ROW_TILE)
    return ROW_TILE // tt, tt


def _in_proj(x, sc, sh, g, w):
    B, T, D = x.shape
    bt, tt = _tile(B, T)
    row = lambda n: pl.BlockSpec((bt, tt, n), lambda i, j: (i, j, 0))
    mod = pl.BlockSpec((bt, 1, D), lambda i, j: (i, 0, 0))
    return pl.pallas_call(
        _in_kernel,
        grid=(B // bt, T // tt),
        in_specs=[row(D), mod, mod,
                  pl.BlockSpec((1, D), lambda i, j: (0, 0)),
                  pl.BlockSpec((D, IN_COLS), lambda i, j: (0, 0))],
        out_specs=[row(RWKV_COLS), row(POOL_WIDTH), row(GATE_COLS)],
        out_shape=[jax.ShapeDtypeStruct((B, T, n), _F32) for n in (RWKV_COLS, POOL_WIDTH, GATE_COLS)],
        compiler_params=_cparams("arbitrary", "arbitrary"),
        name="in_proj",
    )(x, sc, sh, g.reshape(1, D), w)


def _wkv_kernel(r_ref, w_ref, k_ref, kk_ref, b_ref, vt_ref, s0_ref, yt_ref, s_ref):
    @pl.when(pl.program_id(1) == 0)
    def _():
        s_ref[...] = s0_ref[...]

    tc = r_ref.shape[1]
    lane = lax.broadcasted_iota(jnp.int32, yt_ref.shape, 2)
    yt_ref[...] = jnp.zeros(yt_ref.shape, _F32)

    def step(t, carry):
        row = lambda ref: ref[:, pl.ds(t, 1), :]
        sel = lane == t
        s = s_ref[...]
        sa = -jnp.sum(s * row(kk_ref), axis=-1, keepdims=True)
        v_col = jnp.sum(jnp.where(sel, vt_ref[...], 0.0), axis=-1, keepdims=True)
        s = s * row(w_ref) + sa * row(b_ref) + v_col * row(k_ref)
        y_col = jnp.sum(s * row(r_ref), axis=-1, keepdims=True)
        s_ref[...] = s
        yt_ref[...] = jnp.where(sel, y_col, yt_ref[...])
        return carry

    lax.fori_loop(0, tc, step, 0)


def _wkv_scan(r, w, k, kk, b, v, s0, chains):
    B, T, _ = r.shape
    H, N = N_HEADS, HEAD_SIZE
    tc = min(T, WKV_TIME_CHUNK)
    heads = lambda a: a.reshape(B, T, H, N).transpose(0, 2, 1, 3).reshape(B * H, T, N)
    vt = v.reshape(B, T, H, N).transpose(0, 2, 3, 1).reshape(B * H, N, T)
    rows = pl.BlockSpec((chains, tc, N), lambda i, j: (i, j, 0))
    cols = pl.BlockSpec((chains, N, tc), lambda i, j: (i, 0, j))
    state = pl.BlockSpec((chains, N, N), lambda i, j: (i, 0, 0))
    yt, s = pl.pallas_call(
        _wkv_kernel,
        grid=(B * H // chains, T // tc),
        in_specs=[rows] * 5 + [cols, state],
        out_specs=[cols, state],
        out_shape=[jax.ShapeDtypeStruct((B * H, N, T), _F32),
                   jax.ShapeDtypeStruct((B * H, N, N), _F32)],
        compiler_params=_cparams("arbitrary", "arbitrary"),
        name="wkv_scan",
    )(heads(r), heads(w), heads(k), heads(kk), heads(b), vt, s0.reshape(B * H, N, N))
    y = yt.reshape(B, H, N, T).transpose(0, 3, 1, 2).reshape(B, T, H * N)
    return y, s.reshape(B, H, N, N)


def _merge_kernel(x_ref, gt_ref, or_ref, pl_ref, pg_ref, pw_ref, ps_ref, wr_ref, wp_ref, wo_ref, o_ref):
    bt, tt, d = x_ref.shape
    m = bt * tt
    pooled = pl_ref[...].reshape(m, POOL_WIDTH).astype(_BF16)
    z = jnp.concatenate(
        [_dot(pooled[:, g * POOL_GROUP:(g + 1) * POOL_GROUP], pw_ref[g]) for g in range(N_POOL_GROUPS)],
        axis=-1) * ps_ref[...]
    pg = pg_ref[...].reshape(m, GATE_COLS)
    a = _dot(or_ref[...].reshape(m, MIX_WIDTH).astype(_BF16), wr_ref[...])
    b = _dot(z.astype(_BF16), wp_ref[...])
    merged = jax.nn.sigmoid(pg[:, :d]) * a + jax.nn.sigmoid(pg[:, d:]) * b
    out = _dot(merged.astype(_BF16), wo_ref[...]).reshape(bt, tt, d)
    o_ref[...] = x_ref[...] + gt_ref[...] * out


def _merge(x, gt, o_r, pooled, pg, pool_w, pool_scale, w_br_rwkv, w_br_pool, w_out):
    B, T, D = x.shape
    bt, tt = _tile(B, T)
    row = lambda n: pl.BlockSpec((bt, tt, n), lambda i, j: (i, j, 0))
    full = lambda a: pl.BlockSpec(a.shape, lambda i, j: (0,) * a.ndim)
    ps = pool_scale.reshape(1, POOL_WIDTH)
    return pl.pallas_call(
        _merge_kernel,
        grid=(B // bt, T // tt),
        in_specs=[row(D), pl.BlockSpec((bt, 1, D), lambda i, j: (i, 0, 0)),
                  row(MIX_WIDTH), row(POOL_WIDTH), row(GATE_COLS),
                  full(pool_w), full(ps), full(w_br_rwkv), full(w_br_pool), full(w_out)],
        out_specs=row(D),
        out_shape=jax.ShapeDtypeStruct((B, T, D), _F32),
        compiler_params=_cparams("arbitrary", "arbitrary"),
        name="merge",
    )(x, gt, o_r, pooled, pg, pool_w, ps, w_br_rwkv, w_br_pool, w_out)


def _mlp_kernel(x_ref, sc_ref, sh_ref, gt_ref, g_ref, w1_ref, w2_ref, gf_ref, o_ref, *, final_norm):
    bt, tt, d = x_ref.shape
    x = x_ref[...]
    h = _rms(x, g_ref[...]) * (1.0 + sc_ref[...]) + sh_ref[...]
    hb = h.reshape(bt * tt, d).astype(_BF16)
    acc = jnp.zeros((bt * tt, d), _F32)
    for c in range(D_FF // FF_CHUNK):
        u = jnp.maximum(_dot(hb, w1_ref[:, c * FF_CHUNK:(c + 1) * FF_CHUNK]), 0.0)
        acc = acc + _dot((u * u).astype(_BF16), w2_ref[c * FF_CHUNK:(c + 1) * FF_CHUNK, :])
    y = x + gt_ref[...] * acc.reshape(bt, tt, d)
    if final_norm:
        y = _rms(y, gf_ref[...])
    o_ref[...] = y


def _mlp(x, sc, sh, gt, g, w1, w2, g_final, final_norm):
    B, T, D = x.shape
    bt, tt = _tile(B, T)
    row = pl.BlockSpec((bt, tt, D), lambda i, j: (i, j, 0))
    mod = pl.BlockSpec((bt, 1, D), lambda i, j: (i, 0, 0))
    vec = pl.BlockSpec((1, D), lambda i, j: (0, 0))
    return pl.pallas_call(
        functools.partial(_mlp_kernel, final_norm=final_norm),
        grid=(B // bt, T // tt),
        in_specs=[row, mod, mod, mod, vec,
                  pl.BlockSpec((D, D_FF), lambda i, j: (0, 0)),
                  pl.BlockSpec((D_FF, D), lambda i, j: (0, 0)),
                  vec],
        out_specs=row,
        out_shape=jax.ShapeDtypeStruct((B, T, D), _F32),
        compiler_params=_cparams("arbitrary", "arbitrary"),
        name="mlp",
    )(x, sc, sh, gt, g.reshape(1, D), w1, w2, g_final.reshape(1, D))


def _rwkv_prep(pr, shift_prev, v_first, lp):
    B, T, _ = pr.shape
    pr_prev = jnp.concatenate([shift_prev[:, None, :], pr[:, :-1]], axis=1)
    xl = pr + (pr_prev - pr) * lp["mu_shift"]
    r, k, v, xw, xa, xg = jnp.split(xl, RWKV_SPLITS, axis=-1)
    w_logit = lp["w0"] + jnp.tanh(xw) @ lp["w2"]
    decay = jnp.exp(-jnp.exp(-jax.nn.softplus(-w_logit) - 0.5))
    a = jax.nn.sigmoid(lp["a0"] + xa @ lp["a2"])
    g = jax.nn.sigmoid(xg) @ lp["g2"]
    if v_first is None:
        v_first = v
    else:
        v = v + (v_first - v) * jax.nn.sigmoid(lp["v0"] + (v @ lp["v1"]) @ lp["v2"])
    heads = lambda t: t.reshape(B, T, N_HEADS, HEAD_SIZE)
    kk = heads(k * lp["k_k"])
    kk = (kk * lax.rsqrt(jnp.sum(kk * kk, axis=-1, keepdims=True) + 1e-12)).reshape(B, T, MIX_WIDTH)
    k = k * (1 + (a - 1) * lp["k_a"])
    return r, decay, k, v, kk, kk * a, g, v_first


def _rwkv_post(y, r, k, v, g, lp):
    B, T, _ = y.shape
    heads = lambda t: t.reshape(B, T, N_HEADS, HEAD_SIZE)
    yh = heads(y)
    mean = jnp.mean(yh, axis=-1, keepdims=True)
    var = jnp.mean(jnp.square(yh - mean), axis=-1, keepdims=True)
    yn = ((yh - mean) * lax.rsqrt(var + GN_EPS)).reshape(B, T, MIX_WIDTH)
    yn = yn * lp["ln_w"] + lp["ln_b"]
    bonus = jnp.sum(heads(r) * heads(k) * lp["r_k"], axis=-1, keepdims=True) * heads(v)
    return (yn + bonus.reshape(B, T, MIX_WIDTH)) * g


def _pool_prep(pp, pool_prev, pos0):
    B, T, _ = pp.shape
    buf = jnp.concatenate([pool_prev, pp], axis=1)
    cs = jnp.pad(jnp.cumsum(buf, axis=1), ((0, 0), (1, 0), (0, 0)))
    pos = pos0 + jnp.arange(T, dtype=jnp.int32)
    hi = cs[:, POOL_BUF + 1:]
    means = []
    for gi, win in enumerate(POOL_WINDOWS):
        ch = slice(gi * POOL_GROUP, (gi + 1) * POOL_GROUP)
        lo = cs[:, POOL_BUF + 1 - win: POOL_BUF + 1 - win + T, ch]
        cnt = jnp.minimum(pos + 1, win).astype(_F32)[None, :, None]
        means.append((hi[..., ch] - lo) / cnt)
    return jnp.concatenate(means, axis=-1) - pp, buf[:, -POOL_BUF:]


def _trunk(x, mods_mix, mods_mlp, shift0, pool0, wkv0, pos0, W, chains):
    B = x.shape[0]
    shifts, pools, wkvs = [], [], []
    v_first = None
    for l in range(DEPTH):
        lp = {name: arr[l] for name, arr in W.items() if name not in ("v0", "v1", "v2", "norm_final")}
        if l > 0:
            lp.update({name: W[name][l - 1] for name in ("v0", "v1", "v2")})
        sh, sc, gt = (m.reshape(B, 1, D_MODEL) for m in jnp.split(mods_mix[l], 3, axis=-1))
        pr, pp, pg = _in_proj(x, sc, sh, lp["norm_mix"], lp["w_in"])
        r, decay, k, v, kk, b, g, v_first = _rwkv_prep(pr, shift0[l], v_first, lp)
        y, s_wkv = _wkv_scan(r, decay, k, kk, b, v, wkv0[l], chains)
        o_r = _rwkv_post(y, r, k, v, g, lp)
        pooled, s_pool = _pool_prep(pp, pool0[l], pos0)
        x = _merge(x, gt, o_r, pooled, pg, lp["pool_w"], lp["pool_scale"],
                   lp["w_br_rwkv"], lp["w_br_pool"], lp["w_out"])
        sh, sc, gt = (m.reshape(B, 1, D_MODEL) for m in jnp.split(mods_mlp[l], 3, axis=-1))
        x = _mlp(x, sc, sh, gt, lp["norm_mlp"], lp["w_ff1"], lp["w_ff2"], W["norm_final"],
                 final_norm=(l == DEPTH - 1))
        shifts.append(pr[:, -1])
        pools.append(s_pool)
        wkvs.append(s_wkv)
    return x, jnp.stack(shifts), jnp.stack(pools), jnp.stack(wkvs)


def kernel(x_prompt, x_sample, state_shift, state_pool, state_wkv, c_prompt, c_sample, w_ada_mix, b_ada_mix, norm_mix, w_in, mu_shift, w0, w2, a0, a2, g2, v0, v1, v2, k_k, k_a, r_k, ln_w, ln_b, pool_w, pool_scale, w_br_rwkv, w_br_pool, w_out, w_ada_mlp, b_ada_mlp, norm_mlp, w_ff1, w_ff2, norm_final):
    bf = lambda a: a.astype(_BF16)
    W = {
        "norm_mix": norm_mix, "w_in": bf(w_in), "mu_shift": mu_shift, "w0": w0, "w2": w2, "a0": a0,
        "a2": a2, "g2": g2, "v0": v0, "v1": v1, "v2": v2, "k_k": k_k, "k_a": k_a,
        "r_k": r_k, "ln_w": ln_w, "ln_b": ln_b, "pool_w": bf(pool_w), "pool_scale": pool_scale,
        "w_br_rwkv": bf(w_br_rwkv), "w_br_pool": bf(w_br_pool), "w_out": bf(w_out),
        "norm_mlp": norm_mlp, "w_ff1": bf(w_ff1), "w_ff2": bf(w_ff2), "norm_final": norm_final,
    }
    Bp, Bs = x_prompt.shape[0], x_sample.shape[0]
    c_all = jnp.concatenate([c_prompt, c_sample], axis=0)
    mods_mix = _ada_mod(c_all, w_ada_mix, b_ada_mix)
    mods_mlp = _ada_mod(c_all, w_ada_mlp, b_ada_mlp)
    shift0 = jnp.zeros((DEPTH, Bp, RWKV_COLS), _F32)
    pool0 = jnp.zeros((DEPTH, Bp, POOL_BUF, POOL_WIDTH), _F32)
    wkv0 = jnp.zeros((DEPTH, Bp, N_HEADS, HEAD_SIZE, HEAD_SIZE), _F32)
    y_p, shift_p, pool_p, wkv_p = _trunk(x_prompt, mods_mix[:, :Bp], mods_mlp[:, :Bp],
                                         shift0, pool0, wkv0, 0, W, chains=N_HEADS)
    y_s, shift_s, pool_s, wkv_s = _trunk(x_sample, mods_mix[:, Bp:], mods_mlp[:, Bp:],
                                         state_shift, state_pool, state_wkv, PAST_LEN, W, chains=2 * N_HEADS)
    return (y_p, y_s, shift_p, pool_p, wkv_p, shift_s, pool_s, wkv_s)
```

```python
import functools

import jax
import jax.numpy as jnp
from jax import lax
from jax.experimental import pallas as pl
from jax.experimental.pallas import tpu as pltpu

D_MODEL = 1024
DEPTH = 4
PAST_LEN = 16384
MIX_WIDTH = D_MODEL // 2
HEAD_SIZE = 64
N_HEADS = MIX_WIDTH // HEAD_SIZE
D_DECAY_LORA = 64
D_AAA_LORA = 64
D_GATE_LORA = 128
POOL_WIDTH = D_MODEL // 2
POOL_WINDOWS = (2, 4, 8, 16)
N_POOL_GROUPS = len(POOL_WINDOWS)
POOL_GROUP = POOL_WIDTH // N_POOL_GROUPS
POOL_BUF = max(POOL_WINDOWS) - 1
D_FF = 4 * D_MODEL
RWKV_COLS = 3 * MIX_WIDTH + D_DECAY_LORA + D_AAA_LORA + D_GATE_LORA
GATE_COLS = 2 * D_MODEL
IN_COLS = RWKV_COLS + POOL_WIDTH + GATE_COLS
RWKV_SPLITS = (MIX_WIDTH, 2 * MIX_WIDTH, 3 * MIX_WIDTH,
               3 * MIX_WIDTH + D_DECAY_LORA, 3 * MIX_WIDTH + D_DECAY_LORA + D_AAA_LORA)
NORM_EPS = 1e-6
GN_EPS = 64e-5

V7X_VMEM_LIMIT_BYTES = 56 * 1024 * 1024
ROW_TILE = 256
FF_CHUNK = 1024
WKV_TIME_CHUNK = 128

_BF16 = jnp.bfloat16
_F32 = jnp.float32


def _cparams(*sem):
    return pltpu.CompilerParams(dimension_semantics=sem, vmem_limit_bytes=V7X_VMEM_LIMIT_BYTES)


def _dot(a, b):
    return jnp.dot(a, b, preferred_element_type=_F32)


def _rms(x, g):
    return x * lax.rsqrt(jnp.mean(x * x, axis=-1, keepdims=True) + NORM_EPS) * g


def _ada_kernel(c_ref, w_ref, b_ref, o_ref):
    c = c_ref[...]
    s = (c * jax.nn.sigmoid(c)).astype(_BF16)
    o_ref[0] = _dot(s, w_ref[0].astype(_BF16)) + b_ref[0]


def _ada_mod(c, w, b, tn=1024):
    L, D, N = w.shape
    M = c.shape[0]
    return pl.pallas_call(
        _ada_kernel,
        grid=(L, N // tn),
        in_specs=[
            pl.BlockSpec((M, D), lambda l, j: (0, 0)),
            pl.BlockSpec((1, D, tn), lambda l, j: (l, 0, j)),
            pl.BlockSpec((1, 1, tn), lambda l, j: (l, 0, j)),
        ],
        out_specs=pl.BlockSpec((1, M, tn), lambda l, j: (l, 0, j)),
        out_shape=jax.ShapeDtypeStruct((L, M, N), _F32),
        compiler_params=_cparams("arbitrary", "arbitrary"),
        name="ada_mod",
    )(c, w, b.reshape(L, 1, N))


def _in_kernel(x_ref, sc_ref, sh_ref, g_ref, w_ref, pr_ref, pp_ref, pg_ref):
    bt, tt, d = x_ref.shape
    h = _rms(x_ref[...], g_ref[...]) * (1.0 + sc_ref[...]) + sh_ref[...]
    hb = h.reshape(bt * tt, d).astype(_BF16)
    pr_ref[...] = _dot(hb, w_ref[:, :RWKV_COLS]).reshape(bt, tt, RWKV_COLS)
    pp_ref[...] = _dot(hb, w_ref[:, RWKV_COLS:RWKV_COLS + POOL_WIDTH]).reshape(bt, tt, POOL_WIDTH)
    pg_ref[...] = _dot(hb, w_ref[:, RWKV_COLS + POOL_WIDTH:]).reshape(bt, tt, GATE_COLS)


def _tile(B, T):
    tt = min(T, ROW_TILE)
    return ROW_TILE // tt, tt


def _in_proj(x, sc, sh, g, w):
    B, T, D = x.shape
    bt, tt = _tile(B, T)
    row = lambda n: pl.BlockSpec((bt, tt, n), lambda i, j: (i, j, 0))
    mod = pl.BlockSpec((bt, 1, D), lambda i, j: (i, 0, 0))
    return pl.pallas_call(
        _in_kernel,
        grid=(B // bt, T // tt),
        in_specs=[row(D), mod, mod,
                  pl.BlockSpec((1, D), lambda i, j: (0, 0)),
                  pl.BlockSpec((D, IN_COLS), lambda i, j: (0, 0))],
        out_specs=[row(RWKV_COLS), row(POOL_WIDTH), row(GATE_COLS)],
        out_shape=[jax.ShapeDtypeStruct((B, T, n), _F32) for n in (RWKV_COLS, POOL_WIDTH, GATE_COLS)],
        compiler_params=_cparams("arbitrary", "arbitrary"),
        name="in_proj",
    )(x, sc, sh, g.reshape(1, D), w)


def _wkv_kernel(r_ref, w_ref, k_ref, kk_ref, b_ref, vt_ref, s0_ref, yt_ref, s_ref):
    @pl.when(pl.program_id(1) == 0)
    def _():
        s_ref[...] = s0_ref[...]

    tc = r_ref.shape[1]
    lane = lax.broadcasted_iota(jnp.int32, yt_ref.shape, 2)
    yt_ref[...] = jnp.zeros(yt_ref.shape, _F32)

    def step(t, carry):
        row = lambda ref: ref[:, pl.ds(t, 1), :]
        sel = lane == t
        s = s_ref[...]
        sa = -jnp.sum(s * row(kk_ref), axis=-1, keepdims=True)
        v_col = jnp.sum(jnp.where(sel, vt_ref[...], 0.0), axis=-1, keepdims=True)
        s = s * row(w_ref) + sa * row(b_ref) + v_col * row(k_ref)
        y_col = jnp.sum(s * row(r_ref), axis=-1, keepdims=True)
        s_ref[...] = s
        yt_ref[...] = jnp.where(sel, y_col, yt_ref[...])
        return carry

    lax.fori_loop(0, tc, step, 0)


def _wkv_scan(r, w, k, kk, b, v, s0, chains):
    B, T, _ = r.shape
    H, N = N_HEADS, HEAD_SIZE
    tc = min(T, WKV_TIME_CHUNK)
    heads = lambda a: a.reshape(B, T, H, N).transpose(0, 2, 1, 3).reshape(B * H, T, N)
    vt = v.reshape(B, T, H, N).transpose(0, 2, 3, 1).reshape(B * H, N, T)
    rows = pl.BlockSpec((chains, tc, N), lambda i, j: (i, j, 0))
    cols = pl.BlockSpec((chains, N, tc), lambda i, j: (i, 0, j))
    state = pl.BlockSpec((chains, N, N), lambda i, j: (i, 0, 0))
    yt, s = pl.pallas_call(
        _wkv_kernel,
        grid=(B * H // chains, T // tc),
        in_specs=[rows] * 5 + [cols, state],
        out_specs=[cols, state],
        out_shape=[jax.ShapeDtypeStruct((B * H, N, T), _F32),
                   jax.ShapeDtypeStruct((B * H, N, N), _F32)],
        compiler_params=_cparams("arbitrary", "arbitrary"),
        name="wkv_scan",
    )(heads(r), heads(w), heads(k), heads(kk), heads(b), vt, s0.reshape(B * H, N, N))
    y = yt.reshape(B, H, N, T).transpose(0, 3, 1, 2).reshape(B, T, H * N)
    return y, s.reshape(B, H, N, N)


WKV_CHUNK = 64
WKV_SOLVE_BLOCK = 16


def _bmm(a, b, contract=(2, 1)):
    dims = (((contract[0],), (contract[1],)), ((0,), (0,)))
    return lax.dot_general(a.astype(_BF16), b.astype(_BF16), dims, preferred_element_type=_F32)


def _split3(x):
    hi = x.astype(_BF16)
    r1 = x - hi.astype(_F32)
    mid = r1.astype(_BF16)
    lo = (r1 - mid.astype(_F32)).astype(_BF16)
    return hi, mid, lo


def _wkv_chunk_kernel(r_ref, lw_ref, k_ref, v_ref, kk_ref, b_ref, s0_ref, y_ref, s_ref):
    @pl.when(pl.program_id(1) == 0)
    def _():
        s_ref[...] = s0_ref[...]

    G, C, N = r_ref.shape
    ti = lax.broadcasted_iota(jnp.int32, (G, C, C), 1)
    si = lax.broadcasted_iota(jnp.int32, (G, C, C), 2)
    tril_incl = (ti >= si).astype(_BF16)
    eye = (ti == si).astype(_F32)
    strict = ti > si
    blk = (ti // WKV_SOLVE_BLOCK) == (si // WKV_SOLVE_BLOCK)
    t2 = lax.broadcasted_iota(jnp.int32, (G, C, 2 * C), 1)
    s2 = lax.broadcasted_iota(jnp.int32, (G, C, 2 * C), 2)
    mask_kk = (s2 >= C) & (t2 > s2 - C)
    sign_y = jnp.where(s2 < C, -1.0, 1.0).astype(_F32)
    mask_y = ((s2 < C) & (t2 >= s2)) | ((s2 >= C) & (t2 >= s2 - C))

    lw = lw_ref[...]
    k = k_ref[...]
    b = b_ref[...]
    v = v_ref[...]
    s_prev = s_ref[...]
    cl = sum(_bmm(tril_incl, p) for p in _split3(lw))
    cl_last = cl[:, C - 1:C, :]
    e_out = jnp.exp(cl_last - cl)
    e_neg = jnp.exp(-cl)
    lhs = jnp.concatenate([kk_ref[...] * jnp.exp(cl - lw), r_ref[...] * jnp.exp(cl)], axis=1)
    rhs = jnp.concatenate([b * e_neg, k * e_neg], axis=1)
    a = _bmm(lhs, rhs, (2, 2))
    w1 = _bmm(lhs, s_prev, (2, 2))
    rhs_u = w1[:, :C] + _bmm(jnp.where(mask_kk, a[:, :C], 0.0), jnp.concatenate([v, v], axis=1))
    a_kb = jnp.where(strict, a[:, :C, :C], 0.0)
    d = jnp.where(blk, a_kb, 0.0)
    tm = eye - d
    p = d
    n = 1
    while 2 * n < WKV_SOLVE_BLOCK:
        p = _bmm(p, p)
        tm = _bmm(tm, eye + p)
        n *= 2
    x = _bmm(tm, rhs_u)
    mo = _bmm(tm, a_kb - d)
    x = x - _bmm(mo, x)
    p = mo
    n = 1
    while 2 * n < C // WKV_SOLVE_BLOCK:
        p = _bmm(p, p)
        x = x + _bmm(p, x)
        n *= 2
    uv = jnp.concatenate([x, v], axis=1)
    y_ref[...] = w1[:, C:] + _bmm(jnp.where(mask_y, a[:, C:], 0.0) * sign_y, uv)
    kb = jnp.concatenate([-(b * e_out), k * e_out], axis=1)
    s_ref[...] = s_prev * jnp.exp(cl_last) + _bmm(uv, kb, (1, 1))


def _wkv_chunked(r, lw, k, kk, b, v, s0):
    B, T, _ = r.shape
    H, N, C = N_HEADS, HEAD_SIZE, WKV_CHUNK
    heads = lambda a: a.reshape(B, T, H, N).transpose(0, 2, 1, 3).reshape(B * H, T, N)
    rows = pl.BlockSpec((H, C, N), lambda i, j: (i, j, 0))
    state = pl.BlockSpec((H, N, N), lambda i, j: (i, 0, 0))
    y, s = pl.pallas_call(
        _wkv_chunk_kernel,
        grid=(B, T // C),
        in_specs=[rows] * 6 + [state],
        out_specs=[rows, state],
        out_shape=[jax.ShapeDtypeStruct((B * H, T, N), _F32),
                   jax.ShapeDtypeStruct((B * H, N, N), _F32)],
        compiler_params=_cparams("arbitrary", "arbitrary"),
        name="wkv_chunked",
    )(heads(r), heads(lw), heads(k), heads(v), heads(kk), heads(b), s0.reshape(B * H, N, N))
    y = y.reshape(B, H, T, N).transpose(0, 2, 1, 3).reshape(B, T, H * N)
    return y, s.reshape(B, H, N, N)


def _merge_kernel(x_ref, gt_ref, or_ref, pl_ref, pg_ref, pw_ref, ps_ref, wr_ref, wp_ref, wo_ref, o_ref):
    bt, tt, d = x_ref.shape
    m = bt * tt
    pooled = pl_ref[...].reshape(m, POOL_WIDTH).astype(_BF16)
    z = jnp.concatenate(
        [_dot(pooled[:, g * POOL_GROUP:(g + 1) * POOL_GROUP], pw_ref[g]) for g in range(N_POOL_GROUPS)],
        axis=-1) * ps_ref[...]
    pg = pg_ref[...].reshape(m, GATE_COLS)
    a = _dot(or_ref[...].reshape(m, MIX_WIDTH).astype(_BF16), wr_ref[...])
    b = _dot(z.astype(_BF16), wp_ref[...])
    merged = jax.nn.sigmoid(pg[:, :d]) * a + jax.nn.sigmoid(pg[:, d:]) * b
    out = _dot(merged.astype(_BF16), wo_ref[...]).reshape(bt, tt, d)
    o_ref[...] = x_ref[...] + gt_ref[...] * out


def _merge(x, gt, o_r, pooled, pg, pool_w, pool_scale, w_br_rwkv, w_br_pool, w_out):
    B, T, D = x.shape
    bt, tt = _tile(B, T)
    row = lambda n: pl.BlockSpec((bt, tt, n), lambda i, j: (i, j, 0))
    full = lambda a: pl.BlockSpec(a.shape, lambda i, j: (0,) * a.ndim)
    ps = pool_scale.reshape(1, POOL_WIDTH)
    return pl.pallas_call(
        _merge_kernel,
        grid=(B // bt, T // tt),
        in_specs=[row(D), pl.BlockSpec((bt, 1, D), lambda i, j: (i, 0, 0)),
                  row(MIX_WIDTH), row(POOL_WIDTH), row(GATE_COLS),
                  full(pool_w), full(ps), full(w_br_rwkv), full(w_br_pool), full(w_out)],
        out_specs=row(D),
        out_shape=jax.ShapeDtypeStruct((B, T, D), _F32),
        compiler_params=_cparams("arbitrary", "arbitrary"),
        name="merge",
    )(x, gt, o_r, pooled, pg, pool_w, ps, w_br_rwkv, w_br_pool, w_out)


def _mlp_kernel(x_ref, sc_ref, sh_ref, gt_ref, g_ref, w1_ref, w2_ref, gf_ref, o_ref, *, final_norm):
    bt, tt, d = x_ref.shape
    x = x_ref[...]
    h = _rms(x, g_ref[...]) * (1.0 + sc_ref[...]) + sh_ref[...]
    hb = h.reshape(bt * tt, d).astype(_BF16)
    acc = jnp.zeros((bt * tt, d), _F32)
    for c in range(D_FF // FF_CHUNK):
        u = jnp.maximum(_dot(hb, w1_ref[:, c * FF_CHUNK:(c + 1) * FF_CHUNK]), 0.0)
        acc = acc + _dot((u * u).astype(_BF16), w2_ref[c * FF_CHUNK:(c + 1) * FF_CHUNK, :])
    y = x + gt_ref[...] * acc.reshape(bt, tt, d)
    if final_norm:
        y = _rms(y, gf_ref[...])
    o_ref[...] = y


def _mlp(x, sc, sh, gt, g, w1, w2, g_final, final_norm):
    B, T, D = x.shape
    bt, tt = _tile(B, T)
    row = pl.BlockSpec((bt, tt, D), lambda i, j: (i, j, 0))
    mod = pl.BlockSpec((bt, 1, D), lambda i, j: (i, 0, 0))
    vec = pl.BlockSpec((1, D), lambda i, j: (0, 0))
    return pl.pallas_call(
        functools.partial(_mlp_kernel, final_norm=final_norm),
        grid=(B // bt, T // tt),
        in_specs=[row, mod, mod, mod, vec,
                  pl.BlockSpec((D, D_FF), lambda i, j: (0, 0)),
                  pl.BlockSpec((D_FF, D), lambda i, j: (0, 0)),
                  vec],
        out_specs=row,
        out_shape=jax.ShapeDtypeStruct((B, T, D), _F32),
        compiler_params=_cparams("arbitrary", "arbitrary"),
        name="mlp",
    )(x, sc, sh, gt, g.reshape(1, D), w1, w2, g_final.reshape(1, D))


def _rwkv_prep(pr, shift_prev, v_first, lp):
    B, T, _ = pr.shape
    pr_prev = jnp.concatenate([shift_prev[:, None, :], pr[:, :-1]], axis=1)
    xl = pr + (pr_prev - pr) * lp["mu_shift"]
    r, k, v, xw, xa, xg = jnp.split(xl, RWKV_SPLITS, axis=-1)
    w_logit = lp["w0"] + jnp.tanh(xw) @ lp["w2"]
    log_decay = -jnp.exp(-jax.nn.softplus(-w_logit) - 0.5)
    a = jax.nn.sigmoid(lp["a0"] + xa @ lp["a2"])
    g = jax.nn.sigmoid(xg) @ lp["g2"]
    if v_first is None:
        v_first = v
    else:
        v = v + (v_first - v) * jax.nn.sigmoid(lp["v0"] + (v @ lp["v1"]) @ lp["v2"])
    heads = lambda t: t.reshape(B, T, N_HEADS, HEAD_SIZE)
    kk = heads(k * lp["k_k"])
    kk = (kk * lax.rsqrt(jnp.sum(kk * kk, axis=-1, keepdims=True) + 1e-12)).reshape(B, T, MIX_WIDTH)
    k = k * (1 + (a - 1) * lp["k_a"])
    return r, log_decay, k, v, kk, kk * a, g, v_first


def _rwkv_post(y, r, k, v, g, lp):
    B, T, _ = y.shape
    heads = lambda t: t.reshape(B, T, N_HEADS, HEAD_SIZE)
    yh = heads(y)
    mean = jnp.mean(yh, axis=-1, keepdims=True)
    var = jnp.mean(jnp.square(yh - mean), axis=-1, keepdims=True)
    yn = ((yh - mean) * lax.rsqrt(var + GN_EPS)).reshape(B, T, MIX_WIDTH)
    yn = yn * lp["ln_w"] + lp["ln_b"]
    bonus = jnp.sum(heads(r) * heads(k) * lp["r_k"], axis=-1, keepdims=True) * heads(v)
    return (yn + bonus.reshape(B, T, MIX_WIDTH)) * g


def _pool_prep(pp, pool_prev, pos0):
    B, T, _ = pp.shape
    buf = jnp.concatenate([pool_prev, pp], axis=1)
    cs = jnp.pad(jnp.cumsum(buf, axis=1), ((0, 0), (1, 0), (0, 0)))
    pos = pos0 + jnp.arange(T, dtype=jnp.int32)
    hi = cs[:, POOL_BUF + 1:]
    means = []
    for gi, win in enumerate(POOL_WINDOWS):
        ch = slice(gi * POOL_GROUP, (gi + 1) * POOL_GROUP)
        lo = cs[:, POOL_BUF + 1 - win: POOL_BUF + 1 - win + T, ch]
        cnt = jnp.minimum(pos + 1, win).astype(_F32)[None, :, None]
        means.append((hi[..., ch] - lo) / cnt)
    return jnp.concatenate(means, axis=-1) - pp, buf[:, -POOL_BUF:]


def _trunk(x, mods_mix, mods_mlp, shift0, pool0, wkv0, pos0, W, chains):
    B = x.shape[0]
    shifts, pools, wkvs = [], [], []
    v_first = None
    for l in range(DEPTH):
        lp = {name: arr[l] for name, arr in W.items() if name not in ("v0", "v1", "v2", "norm_final")}
        if l > 0:
            lp.update({name: W[name][l - 1] for name in ("v0", "v1", "v2")})
        sh, sc, gt = (m.reshape(B, 1, D_MODEL) for m in jnp.split(mods_mix[l], 3, axis=-1))
        pr, pp, pg = _in_proj(x, sc, sh, lp["norm_mix"], lp["w_in"])
        r, lw, k, v, kk, b, g, v_first = _rwkv_prep(pr, shift0[l], v_first, lp)
        if x.shape[1] % WKV_CHUNK == 0:
            y, s_wkv = _wkv_chunked(r, lw, k, kk, b, v, wkv0[l])
        else:
            y, s_wkv = _wkv_scan(r, jnp.exp(lw), k, kk, b, v, wkv0[l], chains)
        o_r = _rwkv_post(y, r, k, v, g, lp)
        pooled, s_pool = _pool_prep(pp, pool0[l], pos0)
        x = _merge(x, gt, o_r, pooled, pg, lp["pool_w"], lp["pool_scale"],
                   lp["w_br_rwkv"], lp["w_br_pool"], lp["w_out"])
        sh, sc, gt = (m.reshape(B, 1, D_MODEL) for m in jnp.split(mods_mlp[l], 3, axis=-1))
        x = _mlp(x, sc, sh, gt, lp["norm_mlp"], lp["w_ff1"], lp["w_ff2"], W["norm_final"],
                 final_norm=(l == DEPTH - 1))
        shifts.append(pr[:, -1])
        pools.append(s_pool)
        wkvs.append(s_wkv)
    return x, jnp.stack(shifts), jnp.stack(pools), jnp.stack(wkvs)


def kernel(x_prompt, x_sample, state_shift, state_pool, state_wkv, c_prompt, c_sample, w_ada_mix, b_ada_mix, norm_mix, w_in, mu_shift, w0, w2, a0, a2, g2, v0, v1, v2, k_k, k_a, r_k, ln_w, ln_b, pool_w, pool_scale, w_br_rwkv, w_br_pool, w_out, w_ada_mlp, b_ada_mlp, norm_mlp, w_ff1, w_ff2, norm_final):
    bf = lambda a: a.astype(_BF16)
    W = {
        "norm_mix": norm_mix, "w_in": bf(w_in), "mu_shift": mu_shift, "w0": w0, "w2": w2, "a0": a0,
        "a2": a2, "g2": g2, "v0": v0, "v1": v1, "v2": v2, "k_k": k_k, "k_a": k_a,
        "r_k": r_k, "ln_w": ln_w, "ln_b": ln_b, "pool_w": bf(pool_w), "pool_scale": pool_scale,
        "w_br_rwkv": bf(w_br_rwkv), "w_br_pool": bf(w_br_pool), "w_out": bf(w_out),
        "norm_mlp": norm_mlp, "w_ff1": bf(w_ff1), "w_ff2": bf(w_ff2), "norm_final": norm_final,
    }
    Bp, Bs = x_prompt.shape[0], x_sample.shape[0]
    c_all = jnp.concatenate([c_prompt, c_sample], axis=0)
    mods_mix = _ada_mod(c_all, w_ada_mix, b_ada_mix)
    mods_mlp = _ada_mod(c_all, w_ada_mlp, b_ada_mlp)
    shift0 = jnp.zeros((DEPTH, Bp, RWKV_COLS), _F32)
    pool0 = jnp.zeros((DEPTH, Bp, POOL_BUF, POOL_WIDTH), _F32)
    wkv0 = jnp.zeros((DEPTH, Bp, N_HEADS, HEAD_SIZE, HEAD_SIZE), _F32)
    y_p, shift_p, pool_p, wkv_p = _trunk(x_prompt, mods_mix[:, :Bp], mods_mlp[:, :Bp],
                                         shift0, pool0, wkv0, 0, W, chains=N_HEADS)
    y_s, shift_s, pool_s, wkv_s = _trunk(x_sample, mods_mix[:, Bp:], mods_mlp[:, Bp:],
                                         state_shift, state_pool, state_wkv, PAST_LEN, W, chains=2 * N_HEADS)
    return (y_p, y_s, shift_p, pool_p, wkv_p, shift_s, pool_s, wkv_s)
```

```python
import functools

import jax
import jax.numpy as jnp
from jax import lax
from jax.experimental import pallas as pl
from jax.experimental.pallas import tpu as pltpu

D_MODEL = 1024
DEPTH = 4
PAST_LEN = 16384
MIX_WIDTH = D_MODEL // 2
HEAD_SIZE = 64
N_HEADS = MIX_WIDTH // HEAD_SIZE
D_DECAY_LORA = 64
D_AAA_LORA = 64
D_GATE_LORA = 128
POOL_WIDTH = D_MODEL // 2
POOL_WINDOWS = (2, 4, 8, 16)
N_POOL_GROUPS = len(POOL_WINDOWS)
POOL_GROUP = POOL_WIDTH // N_POOL_GROUPS
POOL_BUF = max(POOL_WINDOWS) - 1
POOL_HALO = max(POOL_WINDOWS)
D_FF = 4 * D_MODEL
RWKV_COLS = 3 * MIX_WIDTH + D_DECAY_LORA + D_AAA_LORA + D_GATE_LORA
GATE_COLS = 2 * D_MODEL
IN_COLS = RWKV_COLS + POOL_WIDTH + GATE_COLS
RWKV_SPLITS = (MIX_WIDTH, 2 * MIX_WIDTH, 3 * MIX_WIDTH,
               3 * MIX_WIDTH + D_DECAY_LORA, 3 * MIX_WIDTH + D_DECAY_LORA + D_AAA_LORA)
NORM_EPS = 1e-6
GN_EPS = 64e-5
LOG_DECAY_SCALE = -0.6065306597126334

V7X_VMEM_LIMIT_BYTES = 56 * 1024 * 1024
ROW_TILE = 256
FF_CHUNK = 1024
WKV_CHUNK = 64
WKV_SOLVE_BLOCK = 16
SAMPLE_SEQS_PER_TILE = 8

_BF16 = jnp.bfloat16
_F32 = jnp.float32


def _cparams(*sem):
    return pltpu.CompilerParams(dimension_semantics=sem, vmem_limit_bytes=V7X_VMEM_LIMIT_BYTES)


def _dot(a, b):
    return jnp.dot(a, b, preferred_element_type=_F32)


def _bdot(a, b):
    return _dot(a.astype(_BF16), b.astype(_BF16))


def _bmm(a, b, contract=(2, 1)):
    dims = (((contract[0],), (contract[1],)), ((0,), (0,)))
    return lax.dot_general(a.astype(_BF16), b.astype(_BF16), dims, preferred_element_type=_F32)


def _rms(x, g):
    return x * lax.rsqrt(jnp.mean(x * x, axis=-1, keepdims=True) + NORM_EPS) * g


def _tile(B, T):
    tt = min(T, ROW_TILE)
    return ROW_TILE // tt, tt


def _ada_kernel(c_ref, w_ref, b_ref, o_ref):
    c = c_ref[...]
    o_ref[0] = _bdot(c * jax.nn.sigmoid(c), w_ref[0]) + b_ref[0]


def _ada_mod(c, w, b, tn=1024):
    L, D, N = w.shape
    M = c.shape[0]
    return pl.pallas_call(
        _ada_kernel,
        grid=(L, N // tn),
        in_specs=[
            pl.BlockSpec((M, D), lambda l, j: (0, 0)),
            pl.BlockSpec((1, D, tn), lambda l, j: (l, 0, j)),
            pl.BlockSpec((1, 1, tn), lambda l, j: (l, 0, j)),
        ],
        out_specs=pl.BlockSpec((1, M, tn), lambda l, j: (l, 0, j)),
        out_shape=jax.ShapeDtypeStruct((L, M, N), _F32),
        compiler_params=_cparams("arbitrary", "arbitrary"),
        name="ada_mod",
    )(c, w, b.reshape(L, 1, N))


def _in_kernel(x_ref, sc_ref, sh_ref, g_ref, w_ref, pr_ref, pp_ref, pg_ref):
    bt, tt, d = x_ref.shape
    h = _rms(x_ref[...], g_ref[...]) * (1.0 + sc_ref[...]) + sh_ref[...]
    hb = h.reshape(bt * tt, d).astype(_BF16)
    pr_ref[...] = _dot(hb, w_ref[:, :RWKV_COLS]).reshape(bt, tt, RWKV_COLS)
    pp_ref[...] = _dot(hb, w_ref[:, RWKV_COLS:RWKV_COLS + POOL_WIDTH]).reshape(bt, tt, POOL_WIDTH)
    pg_ref[...] = _dot(hb, w_ref[:, RWKV_COLS + POOL_WIDTH:]).reshape(bt, tt, GATE_COLS)


def _in_proj(x, sc, sh, g, w):
    B, T, D = x.shape
    bt, tt = _tile(B, T)
    row = lambda n: pl.BlockSpec((bt, tt, n), lambda i, j: (i, j, 0))
    mod = pl.BlockSpec((bt, 1, D), lambda i, j: (i, 0, 0))
    return pl.pallas_call(
        _in_kernel,
        grid=(B // bt, T // tt),
        in_specs=[row(D), mod, mod,
                  pl.BlockSpec((1, D), lambda i, j: (0, 0)),
                  pl.BlockSpec((D, IN_COLS), lambda i, j: (0, 0))],
        out_specs=[row(RWKV_COLS), row(POOL_WIDTH), row(GATE_COLS)],
        out_shape=[jax.ShapeDtypeStruct((B, T, n), _F32) for n in (RWKV_COLS, POOL_WIDTH, GATE_COLS)],
        compiler_params=_cparams("arbitrary", "arbitrary"),
        name="in_proj",
    )(x, sc, sh, g.reshape(1, D), w)


def _split3(x):
    hi = x.astype(_BF16)
    r1 = x - hi.astype(_F32)
    mid = r1.astype(_BF16)
    lo = (r1 - mid.astype(_F32)).astype(_BF16)
    return hi, mid, lo


def _wkv_chunk_math(r, lw, k, v, kk, b, s_prev):
    G, C, N = r.shape
    blk_size = min(WKV_SOLVE_BLOCK, C)
    ti = lax.broadcasted_iota(jnp.int32, (G, C, C), 1)
    si = lax.broadcasted_iota(jnp.int32, (G, C, C), 2)
    tril_incl = (ti >= si).astype(_BF16)
    eye = (ti == si).astype(_F32)
    strict = ti > si
    blk = (ti // blk_size) == (si // blk_size)
    t2 = lax.broadcasted_iota(jnp.int32, (G, C, 2 * C), 1)
    s2 = lax.broadcasted_iota(jnp.int32, (G, C, 2 * C), 2)
    mask_kk = (s2 >= C) & (t2 > s2 - C)
    sign_y = jnp.where(s2 < C, -1.0, 1.0).astype(_F32)
    mask_y = ((s2 < C) & (t2 >= s2)) | ((s2 >= C) & (t2 >= s2 - C))

    cl = sum(_bmm(tril_incl, p) for p in _split3(lw))
    cl_last = cl[:, C - 1:C, :]
    e_out = jnp.exp(cl_last - cl)
    e_neg = jnp.exp(-cl)
    lhs = jnp.concatenate([kk * jnp.exp(cl - lw), r * jnp.exp(cl)], axis=1)
    rhs = jnp.concatenate([b * e_neg, k * e_neg], axis=1)
    a = _bmm(lhs, rhs, (2, 2))
    w1 = _bmm(lhs, s_prev, (2, 2))
    rhs_u = w1[:, :C] + _bmm(jnp.where(mask_kk, a[:, :C], 0.0), jnp.concatenate([v, v], axis=1))
    a_kb = jnp.where(strict, a[:, :C, :C], 0.0)
    d = jnp.where(blk, a_kb, 0.0)
    tm = eye - d
    p = d
    n = 1
    while 2 * n < blk_size:
        p = _bmm(p, p)
        tm = _bmm(tm, eye + p)
        n *= 2
    x = _bmm(tm, rhs_u)
    if C > blk_size:
        mo = _bmm(tm, a_kb - d)
        x = x - _bmm(mo, x)
        p = mo
        n = 1
        while 2 * n < C // blk_size:
            p = _bmm(p, p)
            x = x + _bmm(p, x)
            n *= 2
    uv = jnp.concatenate([x, v], axis=1)
    y = w1[:, C:] + _bmm(jnp.where(mask_y, a[:, C:], 0.0) * sign_y, uv)
    kb = jnp.concatenate([-(b * e_out), k * e_out], axis=1)
    s_new = s_prev * jnp.exp(cl_last) + _bmm(uv, kb, (1, 1))
    return y, s_new


def _mix_kernel(*refs, has_vfirst):
    it = iter(refs)
    pr_ref, sh_ref = next(it), next(it)
    vf_ref = next(it) if has_vfirst else None
    s0_ref, mu_ref, w0_ref, w2_ref, a0_ref, a2_ref, g2_ref = (next(it) for _ in range(7))
    v0_ref, v1_ref, v2_ref = (next(it) for _ in range(3)) if has_vfirst else (None, None, None)
    kk_ref, ka_ref, rk_ref, lnw_ref, lnb_ref = (next(it) for _ in range(5))
    o_ref = next(it)
    vfo_ref = None if has_vfirst else next(it)
    s_ref, prev_ref = next(it), next(it)

    @pl.when(pl.program_id(1) == 0)
    def _():
        prev_ref[...] = sh_ref[...]
        s_ref[...] = s0_ref[...]

    n_seq, L, _ = pr_ref.shape
    rows = n_seq * L
    H, N, M = N_HEADS, HEAD_SIZE, MIX_WIDTH
    pr = pr_ref[...]
    x = pr.reshape(rows, RWKV_COLS)
    init = jnp.broadcast_to(prev_ref[...], (n_seq, L, RWKV_COLS)).reshape(rows, RWKV_COLS)
    first = (lax.broadcasted_iota(jnp.int32, (rows, RWKV_COLS), 0) & (L - 1)) == 0
    prev = jnp.where(first, init, pltpu.roll(x, 1, 0))
    prev_ref[...] = pr[:, L - 1:L, :]
    xl = x + (prev - x) * mu_ref[...]
    r, k, v = xl[:, :M], xl[:, M:2 * M], xl[:, 2 * M:3 * M]
    xw, xa, xg = (xl[:, RWKV_SPLITS[2]:RWKV_SPLITS[3]], xl[:, RWKV_SPLITS[3]:RWKV_SPLITS[4]],
                  xl[:, RWKV_SPLITS[4]:])
    lw = LOG_DECAY_SCALE * jax.nn.sigmoid(w0_ref[...] + _bdot(jnp.tanh(xw), w2_ref[...]))
    a = jax.nn.sigmoid(a0_ref[...] + _bdot(xa, a2_ref[...]))
    g = _bdot(jax.nn.sigmoid(xg), g2_ref[...])
    if has_vfirst:
        vf = vf_ref[...].reshape(rows, M)
        v = v + (vf - v) * jax.nn.sigmoid(v0_ref[...] + _bdot(_bdot(v, v1_ref[...]), v2_ref[...]))
    else:
        vfo_ref[...] = v.reshape(n_seq, L, M)

    def split_heads(t):
        t3 = t.reshape(n_seq, L, M)
        return jnp.concatenate([t3[:, :, h * N:(h + 1) * N] for h in range(H)], axis=0)

    def per_chain(ref):
        return jnp.broadcast_to(ref[...][:, None], (H, n_seq, 1, N)).reshape(H * n_seq, 1, N)

    rh, lwh, kh, vh, ah, gh = (split_heads(t) for t in (r, lw, k, v, a, g))
    kk = kh * per_chain(kk_ref)
    kk = kk * lax.rsqrt(jnp.sum(kk * kk, axis=-1, keepdims=True) + 1e-12)
    kh = kh * (1.0 + (ah - 1.0) * per_chain(ka_ref))
    y, s_new = _wkv_chunk_math(rh, lwh, kh, vh, kk, kk * ah, s_ref[...].reshape(H * n_seq, N, N))
    s_ref[...] = s_new.reshape(H, n_seq, N, N)
    yc = y - jnp.mean(y, axis=-1, keepdims=True)
    var = jnp.mean(yc * yc, axis=-1, keepdims=True)
    yn = yc * lax.rsqrt(var + GN_EPS) * per_chain(lnw_ref) + per_chain(lnb_ref)
    bonus = jnp.sum(rh * kh * per_chain(rk_ref), axis=-1, keepdims=True) * vh
    o = (yn + bonus) * gh
    o_ref[...] = jnp.concatenate([o[h * n_seq:(h + 1) * n_seq] for h in range(H)], axis=-1)


def _time_mix(pr, shift_prev, v_first, s0, lp, n_seq, seq_len):
    B, T, _ = pr.shape
    H, N, M = N_HEADS, HEAD_SIZE, MIX_WIDTH
    has_vfirst = v_first is not None
    row = lambda n: pl.BlockSpec((n_seq, seq_len, n), lambda i, j: (i, j, 0))
    full = lambda a: pl.BlockSpec(a.shape, lambda i, j: (0,) * a.ndim)
    state = pl.BlockSpec((H, n_seq, N, N), lambda i, j: (0, i, 0, 0))
    vec = lambda a: a.reshape(1, -1)
    hp = lambda a: a.reshape(H, 1, N)
    params = [vec(lp["mu_shift"]), vec(lp["w0"]), lp["w2"], vec(lp["a0"]), lp["a2"], lp["g2"]]
    if has_vfirst:
        params += [vec(lp["v0"]), lp["v1"], lp["v2"]]
    params += [hp(lp["k_k"]), hp(lp["k_a"]), hp(lp["r_k"]), hp(lp["ln_w"]), hp(lp["ln_b"])]
    args = [pr, shift_prev.reshape(B, 1, RWKV_COLS)] + ([v_first] if has_vfirst else [])
    args += [s0.transpose(1, 0, 2, 3)] + params
    in_specs = [row(RWKV_COLS), pl.BlockSpec((n_seq, 1, RWKV_COLS), lambda i, j: (i, 0, 0))]
    in_specs += [row(M)] if has_vfirst else []
    in_specs += [state] + [full(p) for p in params]
    out_specs = [row(M)] + ([] if has_vfirst else [row(M)]) + [state]
    out_shape = [jax.ShapeDtypeStruct((B, T, M), _F32)] * (1 if has_vfirst else 2)
    out_shape += [jax.ShapeDtypeStruct((H, B, N, N), _F32)]
    outs = pl.pallas_call(
        functools.partial(_mix_kernel, has_vfirst=has_vfirst),
        grid=(B // n_seq, T // seq_len),
        in_specs=in_specs,
        out_specs=out_specs,
        out_shape=out_shape,
        scratch_shapes=[pltpu.VMEM((n_seq, 1, RWKV_COLS), _F32)],
        compiler_params=_cparams("arbitrary", "arbitrary"),
        name="time_mix",
    )(*args)
    return outs[0], (v_first if has_vfirst else outs[1]), outs[-1].transpose(1, 0, 2, 3)


def _merge_kernel(x_ref, gt_ref, or_ref, pp_ref, pg_ref, ps0_ref, pw_ref, psc_ref, wr_ref, wp_ref, wo_ref,
                  o_ref, ps_ref, *, pos0):
    bt, tt, d = x_ref.shape
    m = bt * tt
    j = pl.program_id(1)

    @pl.when(j == 0)
    def _():
        ps_ref[...] = ps0_ref[...]

    pp = pp_ref[...]
    ext3 = jnp.concatenate([ps_ref[...], pp], axis=1)
    ps_ref[...] = ext3[:, tt:tt + POOL_HALO, :]
    ext = ext3.reshape(bt * (POOL_HALO + tt), POOL_WIDTH)
    pos = pos0 + j * tt + lax.broadcasted_iota(jnp.int32, (bt, tt, POOL_GROUP), 1)
    acc, win, means = ext, 1, []
    for gi, target in enumerate(POOL_WINDOWS):
        while win < target:
            acc = acc + pltpu.roll(acc, win, 0)
            win *= 2
        tot = acc.reshape(bt, POOL_HALO + tt, POOL_WIDTH)[:, POOL_HALO:, gi * POOL_GROUP:(gi + 1) * POOL_GROUP]
        means.append(tot / jnp.minimum(pos + 1, target).astype(_F32))
    pooled = (jnp.concatenate(means, axis=-1) - pp).reshape(m, POOL_WIDTH).astype(_BF16)
    z = jnp.concatenate(
        [_dot(pooled[:, g * POOL_GROUP:(g + 1) * POOL_GROUP], pw_ref[g]) for g in range(N_POOL_GROUPS)],
        axis=-1) * psc_ref[...]
    pg = pg_ref[...].reshape(m, GATE_COLS)
    a = _bdot(or_ref[...].reshape(m, MIX_WIDTH), wr_ref[...])
    b = _bdot(z, wp_ref[...])
    merged = jax.nn.sigmoid(pg[:, :d]) * a + jax.nn.sigmoid(pg[:, d:]) * b
    o_ref[...] = x_ref[...] + gt_ref[...] * _bdot(merged, wo_ref[...]).reshape(bt, tt, d)


def _merge(x, gt, o_r, pp, pg, pool_prev, pos0, pool_w, pool_scale, w_br_rwkv, w_br_pool, w_out):
    B, T, D = x.shape
    bt, tt = _tile(B, T)
    row = lambda n: pl.BlockSpec((bt, tt, n), lambda i, j: (i, j, 0))
    full = lambda a: pl.BlockSpec(a.shape, lambda i, j: (0,) * a.ndim)
    halo = pl.BlockSpec((bt, POOL_HALO, POOL_WIDTH), lambda i, j: (i, 0, 0))
    ps = pool_scale.reshape(1, POOL_WIDTH)
    ps0 = jnp.pad(pool_prev, ((0, 0), (POOL_HALO - POOL_BUF, 0), (0, 0)))
    x_new, pool_new = pl.pallas_call(
        functools.partial(_merge_kernel, pos0=pos0),
        grid=(B // bt, T // tt),
        in_specs=[row(D), pl.BlockSpec((bt, 1, D), lambda i, j: (i, 0, 0)),
                  row(MIX_WIDTH), row(POOL_WIDTH), row(GATE_COLS), halo,
                  full(pool_w), full(ps), full(w_br_rwkv), full(w_br_pool), full(w_out)],
        out_specs=[row(D), halo],
        out_shape=[jax.ShapeDtypeStruct((B, T, D), _F32),
                   jax.ShapeDtypeStruct((B, POOL_HALO, POOL_WIDTH), _F32)],
        compiler_params=_cparams("arbitrary", "arbitrary"),
        name="merge",
    )(x, gt, o_r, pp, pg, ps0, pool_w, ps, w_br_rwkv, w_br_pool, w_out)
    return x_new, pool_new[:, POOL_HALO - POOL_BUF:]


def _mlp_kernel(x_ref, sc_ref, sh_ref, gt_ref, g_ref, w1_ref, w2_ref, gf_ref, o_ref, *, final_norm):
    bt, tt, d = x_ref.shape
    x = x_ref[...]
    h = _rms(x, g_ref[...]) * (1.0 + sc_ref[...]) + sh_ref[...]
    hb = h.reshape(bt * tt, d).astype(_BF16)
    acc = jnp.zeros((bt * tt, d), _F32)
    for c in range(D_FF // FF_CHUNK):
        u = jnp.maximum(_dot(hb, w1_ref[:, c * FF_CHUNK:(c + 1) * FF_CHUNK]), 0.0)
        acc = acc + _bdot(u * u, w2_ref[c * FF_CHUNK:(c + 1) * FF_CHUNK, :])
    y = x + gt_ref[...] * acc.reshape(bt, tt, d)
    if final_norm:
        y = _rms(y, gf_ref[...])
    o_ref[...] = y


def _mlp(x, sc, sh, gt, g, w1, w2, g_final, final_norm):
    B, T, D = x.shape
    bt, tt = _tile(B, T)
    row = pl.BlockSpec((bt, tt, D), lambda i, j: (i, j, 0))
    mod = pl.BlockSpec((bt, 1, D), lambda i, j: (i, 0, 0))
    vec = pl.BlockSpec((1, D), lambda i, j: (0, 0))
    return pl.pallas_call(
        functools.partial(_mlp_kernel, final_norm=final_norm),
        grid=(B // bt, T // tt),
        in_specs=[row, mod, mod, mod, vec,
                  pl.BlockSpec((D, D_FF), lambda i, j: (0, 0)),
                  pl.BlockSpec((D_FF, D), lambda i, j: (0, 0)),
                  vec],
        out_specs=row,
        out_shape=jax.ShapeDtypeStruct((B, T, D), _F32),
        compiler_params=_cparams("arbitrary", "arbitrary"),
        name="mlp",
    )(x, sc, sh, gt, g.reshape(1, D), w1, w2, g_final.reshape(1, D))


def _trunk(x, mods_mix, mods_mlp, shift0, pool0, wkv0, pos0, W, mix_tile):
    B = x.shape[0]
    shifts, pools, wkvs = [], [], []
    v_first = None
    for l in range(DEPTH):
        lp = {name: arr[l] for name, arr in W.items() if name not in ("v0", "v1", "v2", "norm_final")}
        if l > 0:
            lp.update({name: W[name][l - 1] for name in ("v0", "v1", "v2")})
        sh, sc, gt = (m.reshape(B, 1, D_MODEL) for m in jnp.split(mods_mix[l], 3, axis=-1))
        pr, pp, pg = _in_proj(x, sc, sh, lp["norm_mix"], lp["w_in"])
        o_r, v_first, s_wkv = _time_mix(pr, shift0[l], v_first, wkv0[l], lp, *mix_tile)
        x, s_pool = _merge(x, gt, o_r, pp, pg, pool0[l], pos0, lp["pool_w"], lp["pool_scale"],
                           lp["w_br_rwkv"], lp["w_br_pool"], lp["w_out"])
        sh, sc, gt = (m.reshape(B, 1, D_MODEL) for m in jnp.split(mods_mlp[l], 3, axis=-1))
        x = _mlp(x, sc, sh, gt, lp["norm_mlp"], lp["w_ff1"], lp["w_ff2"], W["norm_final"],
                 final_norm=(l == DEPTH - 1))
        shifts.append(pr[:, -1])
        pools.append(s_pool)
        wkvs.append(s_wkv)
    return x, jnp.stack(shifts), jnp.stack(pools), jnp.stack(wkvs)


def kernel(x_prompt, x_sample, state_shift, state_pool, state_wkv, c_prompt, c_sample, w_ada_mix, b_ada_mix, norm_mix, w_in, mu_shift, w0, w2, a0, a2, g2, v0, v1, v2, k_k, k_a, r_k, ln_w, ln_b, pool_w, pool_scale, w_br_rwkv, w_br_pool, w_out, w_ada_mlp, b_ada_mlp, norm_mlp, w_ff1, w_ff2, norm_final):
    bf = lambda a: a.astype(_BF16)
    W = {
        "norm_mix": norm_mix, "w_in": bf(w_in), "mu_shift": mu_shift, "w0": w0, "w2": w2, "a0": a0,
        "a2": a2, "g2": g2, "v0": v0, "v1": v1, "v2": v2, "k_k": k_k, "k_a": k_a,
        "r_k": r_k, "ln_w": ln_w, "ln_b": ln_b, "pool_w": bf(pool_w), "pool_scale": pool_scale,
        "w_br_rwkv": bf(w_br_rwkv), "w_br_pool": bf(w_br_pool), "w_out": bf(w_out),
        "norm_mlp": norm_mlp, "w_ff1": bf(w_ff1), "w_ff2": bf(w_ff2), "norm_final": norm_final,
    }
    Bp = x_prompt.shape[0]
    c_all = jnp.concatenate([c_prompt, c_sample], axis=0)
    mods_mix = _ada_mod(c_all, w_ada_mix, b_ada_mix)
    mods_mlp = _ada_mod(c_all, w_ada_mlp, b_ada_mlp)
    shift0 = jnp.zeros((DEPTH, Bp, RWKV_COLS), _F32)
    pool0 = jnp.zeros((DEPTH, Bp, POOL_BUF, POOL_WIDTH), _F32)
    wkv0 = jnp.zeros((DEPTH, Bp, N_HEADS, HEAD_SIZE, HEAD_SIZE), _F32)
    y_p, shift_p, pool_p, wkv_p = _trunk(x_prompt, mods_mix[:, :Bp], mods_mlp[:, :Bp],
                                         shift0, pool0, wkv0, 0, W, mix_tile=(1, WKV_CHUNK))
    y_s, shift_s, pool_s, wkv_s = _trunk(x_sample, mods_mix[:, Bp:], mods_mlp[:, Bp:],
                                         state_shift, state_pool, state_wkv, PAST_LEN, W,
                                         mix_tile=(SAMPLE_SEQS_PER_TILE, x_sample.shape[1]))
    return (y_p, y_s, shift_p, pool_p, wkv_p, shift_s, pool_s, wkv_s)
```

```python
import functools

import jax
import jax.numpy as jnp
from jax import lax
from jax.experimental import pallas as pl
from jax.experimental.pallas import tpu as pltpu

D_MODEL = 1024
DEPTH = 4
PAST_LEN = 16384
MIX_WIDTH = D_MODEL // 2
HEAD_SIZE = 64
N_HEADS = MIX_WIDTH // HEAD_SIZE
D_DECAY_LORA = 64
D_AAA_LORA = 64
D_GATE_LORA = 128
POOL_WIDTH = D_MODEL // 2
POOL_WINDOWS = (2, 4, 8, 16)
N_POOL_GROUPS = len(POOL_WINDOWS)
POOL_GROUP = POOL_WIDTH // N_POOL_GROUPS
POOL_BUF = max(POOL_WINDOWS) - 1
POOL_HALO = max(POOL_WINDOWS)
D_FF = 4 * D_MODEL
RWKV_COLS = 3 * MIX_WIDTH + D_DECAY_LORA + D_AAA_LORA + D_GATE_LORA
GATE_COLS = 2 * D_MODEL
IN_COLS = RWKV_COLS + POOL_WIDTH + GATE_COLS
RWKV_SPLITS = (MIX_WIDTH, 2 * MIX_WIDTH, 3 * MIX_WIDTH,
               3 * MIX_WIDTH + D_DECAY_LORA, 3 * MIX_WIDTH + D_DECAY_LORA + D_AAA_LORA)
NORM_EPS = 1e-6
GN_EPS = 64e-5
LOG_DECAY_SCALE = -0.6065306597126334

V7X_VMEM_LIMIT_BYTES = 56 * 1024 * 1024
ROW_TILE = 256
FF_CHUNK = 1024
WKV_CHUNK = 64
WKV_SOLVE_BLOCK = 16
SAMPLE_SEQS_PER_TILE = 8
PROMPT_SEQS_PER_TILE = 4

_BF16 = jnp.bfloat16
_F32 = jnp.float32


def _cparams(*sem):
    return pltpu.CompilerParams(dimension_semantics=sem, vmem_limit_bytes=V7X_VMEM_LIMIT_BYTES)


def _dot(a, b):
    return jnp.dot(a, b, preferred_element_type=_F32)


def _bdot(a, b):
    return _dot(a.astype(_BF16), b.astype(_BF16))


def _bmm(a, b, contract=(2, 1)):
    dims = (((contract[0],), (contract[1],)), ((0,), (0,)))
    return lax.dot_general(a.astype(_BF16), b.astype(_BF16), dims, preferred_element_type=_F32)


def _rms(x, g):
    return x * lax.rsqrt(jnp.mean(x * x, axis=-1, keepdims=True) + NORM_EPS) * g


def _tile(B, T):
    tt = min(T, ROW_TILE)
    return ROW_TILE // tt, tt


def _ada_kernel(c_ref, w_ref, b_ref, o_ref):
    c = c_ref[...]
    o_ref[0] = _bdot(c * jax.nn.sigmoid(c), w_ref[0]) + b_ref[0]


def _ada_mod(c, w, b, tn=1024):
    L, D, N = w.shape
    M = c.shape[0]
    return pl.pallas_call(
        _ada_kernel,
        grid=(L, N // tn),
        in_specs=[
            pl.BlockSpec((M, D), lambda l, j: (0, 0)),
            pl.BlockSpec((1, D, tn), lambda l, j: (l, 0, j)),
            pl.BlockSpec((1, 1, tn), lambda l, j: (l, 0, j)),
        ],
        out_specs=pl.BlockSpec((1, M, tn), lambda l, j: (l, 0, j)),
        out_shape=jax.ShapeDtypeStruct((L, M, N), _F32),
        compiler_params=_cparams("arbitrary", "arbitrary"),
        name="ada_mod",
    )(c, w, b.reshape(L, 1, N))


def _ada_norm(x_ref, g_ref, sc_ref, sh_ref):
    bt, tt, d = x_ref.shape
    h = _rms(x_ref[...], g_ref[...]) * (1.0 + sc_ref[...]) + sh_ref[...]
    return h.reshape(bt * tt, d).astype(_BF16)


def _in_kernel(x_ref, sc_ref, sh_ref, g_ref, w_ref, pr_ref):
    pr_ref[...] = _dot(_ada_norm(x_ref, g_ref, sc_ref, sh_ref), w_ref[...]).reshape(pr_ref.shape)


def _in_proj(x, sc, sh, g, w):
    B, T, D = x.shape
    bt, tt = _tile(B, T)
    row = lambda n: pl.BlockSpec((bt, tt, n), lambda i, j: (i, j, 0))
    mod = pl.BlockSpec((bt, 1, D), lambda i, j: (i, 0, 0))
    return pl.pallas_call(
        _in_kernel,
        grid=(B // bt, T // tt),
        in_specs=[row(D), mod, mod,
                  pl.BlockSpec((1, D), lambda i, j: (0, 0)),
                  pl.BlockSpec((D, RWKV_COLS), lambda i, j: (0, 0))],
        out_specs=row(RWKV_COLS),
        out_shape=jax.ShapeDtypeStruct((B, T, RWKV_COLS), _F32),
        compiler_params=_cparams("arbitrary", "arbitrary"),
        name="in_proj",
    )(x, sc, sh, g.reshape(1, D), w)


def _split3(x):
    hi = x.astype(_BF16)
    r1 = x - hi.astype(_F32)
    mid = r1.astype(_BF16)
    lo = (r1 - mid.astype(_F32)).astype(_BF16)
    return hi, mid, lo


def _wkv_chunk_math(r, lw, k, v, kk, b, s_prev):
    G, C, N = r.shape
    blk_size = min(WKV_SOLVE_BLOCK, C)
    ti = lax.broadcasted_iota(jnp.int32, (G, C, C), 1)
    si = lax.broadcasted_iota(jnp.int32, (G, C, C), 2)
    tril_incl = (ti >= si).astype(_BF16)
    eye = (ti == si).astype(_F32)
    strict = ti > si
    blk = (ti // blk_size) == (si // blk_size)
    t2 = lax.broadcasted_iota(jnp.int32, (G, C, 2 * C), 1)
    s2 = lax.broadcasted_iota(jnp.int32, (G, C, 2 * C), 2)
    mask_kk = (s2 >= C) & (t2 > s2 - C)
    sign_y = jnp.where(s2 < C, -1.0, 1.0).astype(_F32)
    mask_y = ((s2 < C) & (t2 >= s2)) | ((s2 >= C) & (t2 >= s2 - C))

    cl = sum(_bmm(tril_incl, p) for p in _split3(lw))
    cl_last = cl[:, C - 1:C, :]
    e_out = jnp.exp(cl_last - cl)
    e_neg = jnp.exp(-cl)
    lhs = jnp.concatenate([kk * jnp.exp(cl - lw), r * jnp.exp(cl)], axis=1)
    rhs = jnp.concatenate([b * e_neg, k * e_neg], axis=1)
    a = _bmm(lhs, rhs, (2, 2))
    w1 = _bmm(lhs, s_prev, (2, 2))
    rhs_u = w1[:, :C] + _bmm(jnp.where(mask_kk, a[:, :C], 0.0), jnp.concatenate([v, v], axis=1))
    a_kb = jnp.where(strict, a[:, :C, :C], 0.0)
    d = jnp.where(blk, a_kb, 0.0)
    tm = eye - d
    p = d
    n = 1
    while 2 * n < blk_size:
        p = _bmm(p, p)
        tm = _bmm(tm, eye + p)
        n *= 2
    x = _bmm(tm, rhs_u)
    if C > blk_size:
        mo = _bmm(tm, a_kb - d)
        x = x - _bmm(mo, x)
        p = mo
        n = 1
        while 2 * n < C // blk_size:
            p = _bmm(p, p)
            x = x + _bmm(p, x)
            n *= 2
    uv = jnp.concatenate([x, v], axis=1)
    y = w1[:, C:] + _bmm(jnp.where(mask_y, a[:, C:], 0.0) * sign_y, uv)
    kb = jnp.concatenate([-(b * e_out), k * e_out], axis=1)
    s_new = s_prev * jnp.exp(cl_last) + _bmm(uv, kb, (1, 1))
    return y, s_new


def _mix_kernel(*refs, has_vfirst):
    it = iter(refs)
    pr_ref, sh_ref = next(it), next(it)
    vf_ref = next(it) if has_vfirst else None
    s0_ref, mu_ref, w0_ref, w2_ref, a0_ref, a2_ref, g2_ref = (next(it) for _ in range(7))
    v0_ref, v1_ref, v2_ref = (next(it) for _ in range(3)) if has_vfirst else (None, None, None)
    kk_ref, ka_ref, rk_ref, lnw_ref, lnb_ref = (next(it) for _ in range(5))
    o_ref = next(it)
    vfo_ref = None if has_vfirst else next(it)
    s_ref, prev_ref = next(it), next(it)

    @pl.when(pl.program_id(1) == 0)
    def _():
        prev_ref[...] = sh_ref[...]
        s_ref[...] = s0_ref[...]

    n_seq, L, _ = pr_ref.shape
    rows = n_seq * L
    H, N, M = N_HEADS, HEAD_SIZE, MIX_WIDTH
    pr = pr_ref[...]
    x = pr.reshape(rows, RWKV_COLS)
    init = jnp.broadcast_to(prev_ref[...], (n_seq, L, RWKV_COLS)).reshape(rows, RWKV_COLS)
    first = (lax.broadcasted_iota(jnp.int32, (rows, RWKV_COLS), 0) & (L - 1)) == 0
    prev = jnp.where(first, init, pltpu.roll(x, 1, 0))
    prev_ref[...] = pr[:, L - 1:L, :]
    xl = x + (prev - x) * mu_ref[...]
    r, k, v = xl[:, :M], xl[:, M:2 * M], xl[:, 2 * M:3 * M]
    xw, xa, xg = (xl[:, RWKV_SPLITS[2]:RWKV_SPLITS[3]], xl[:, RWKV_SPLITS[3]:RWKV_SPLITS[4]],
                  xl[:, RWKV_SPLITS[4]:])
    lw = LOG_DECAY_SCALE * jax.nn.sigmoid(w0_ref[...] + _bdot(jnp.tanh(xw), w2_ref[...]))
    a = jax.nn.sigmoid(a0_ref[...] + _bdot(xa, a2_ref[...]))
    g = _bdot(jax.nn.sigmoid(xg), g2_ref[...])
    if has_vfirst:
        vf = vf_ref[...].reshape(rows, M)
        v = v + (vf - v) * jax.nn.sigmoid(v0_ref[...] + _bdot(_bdot(v, v1_ref[...]), v2_ref[...]))
    else:
        vfo_ref[...] = v.reshape(n_seq, L, M)

    def split_heads(t):
        t3 = t.reshape(n_seq, L, M)
        return jnp.concatenate([t3[:, :, h * N:(h + 1) * N] for h in range(H)], axis=0)

    def per_chain(ref):
        return jnp.broadcast_to(ref[...][:, None], (H, n_seq, 1, N)).reshape(H * n_seq, 1, N)

    rh, lwh, kh, vh, ah, gh = (split_heads(t) for t in (r, lw, k, v, a, g))
    kk = kh * per_chain(kk_ref)
    kk = kk * lax.rsqrt(jnp.sum(kk * kk, axis=-1, keepdims=True) + 1e-12)
    kh = kh * (1.0 + (ah - 1.0) * per_chain(ka_ref))
    y, s_new = _wkv_chunk_math(rh, lwh, kh, vh, kk, kk * ah, s_ref[...].reshape(H * n_seq, N, N))
    s_ref[...] = s_new.reshape(H, n_seq, N, N)
    yc = y - jnp.mean(y, axis=-1, keepdims=True)
    var = jnp.mean(yc * yc, axis=-1, keepdims=True)
    yn = yc * lax.rsqrt(var + GN_EPS) * per_chain(lnw_ref) + per_chain(lnb_ref)
    bonus = jnp.sum(rh * kh * per_chain(rk_ref), axis=-1, keepdims=True) * vh
    o = (yn + bonus) * gh
    o_ref[...] = jnp.concatenate([o[h * n_seq:(h + 1) * n_seq] for h in range(H)], axis=-1)


def _time_mix(pr, shift_prev, v_first, s0, lp, n_seq, seq_len):
    B, T, _ = pr.shape
    H, N, M = N_HEADS, HEAD_SIZE, MIX_WIDTH
    has_vfirst = v_first is not None
    row = lambda n: pl.BlockSpec((n_seq, seq_len, n), lambda i, j: (i, j, 0))
    full = lambda a: pl.BlockSpec(a.shape, lambda i, j: (0,) * a.ndim)
    state = pl.BlockSpec((H, n_seq, N, N), lambda i, j: (0, i, 0, 0))
    vec = lambda a: a.reshape(1, -1)
    hp = lambda a: a.reshape(H, 1, N)
    params = [vec(lp["mu_shift"]), vec(lp["w0"]), lp["w2"], vec(lp["a0"]), lp["a2"], lp["g2"]]
    if has_vfirst:
        params += [vec(lp["v0"]), lp["v1"], lp["v2"]]
    params += [hp(lp["k_k"]), hp(lp["k_a"]), hp(lp["r_k"]), hp(lp["ln_w"]), hp(lp["ln_b"])]
    args = [pr, shift_prev.reshape(B, 1, RWKV_COLS)] + ([v_first] if has_vfirst else [])
    args += [s0.transpose(1, 0, 2, 3)] + params
    in_specs = [row(RWKV_COLS), pl.BlockSpec((n_seq, 1, RWKV_COLS), lambda i, j: (i, 0, 0))]
    in_specs += [row(M)] if has_vfirst else []
    in_specs += [state] + [full(p) for p in params]
    out_specs = [row(M)] + ([] if has_vfirst else [row(M)]) + [state]
    out_shape = [jax.ShapeDtypeStruct((B, T, M), _F32)] * (1 if has_vfirst else 2)
    out_shape += [jax.ShapeDtypeStruct((H, B, N, N), _F32)]
    outs = pl.pallas_call(
        functools.partial(_mix_kernel, has_vfirst=has_vfirst),
        grid=(B // n_seq, T // seq_len),
        in_specs=in_specs,
        out_specs=out_specs,
        out_shape=out_shape,
        scratch_shapes=[pltpu.VMEM((n_seq, 1, RWKV_COLS), _F32)],
        compiler_params=_cparams("arbitrary", "arbitrary"),
        name="time_mix",
    )(*args)
    return outs[0], (v_first if has_vfirst else outs[1]), outs[-1].transpose(1, 0, 2, 3)


def _merge_kernel(x_ref, sc_ref, sh_ref, gt_ref, g_ref, or_ref, ps0_ref, wi_ref, pw_ref, psc_ref, wr_ref,
                  wp_ref, wo_ref, o_ref, ps_ref, *, pos0):
    bt, tt, d = x_ref.shape
    m = bt * tt
    j = pl.program_id(1)

    @pl.when(j == 0)
    def _():
        ps_ref[...] = ps0_ref[...]

    ppg = _dot(_ada_norm(x_ref, g_ref, sc_ref, sh_ref), wi_ref[...])
    pp = ppg[:, :POOL_WIDTH].reshape(bt, tt, POOL_WIDTH)
    ext3 = jnp.concatenate([ps_ref[...], pp], axis=1)
    ps_ref[...] = ext3[:, tt:tt + POOL_HALO, :]
    ext = ext3.reshape(bt * (POOL_HALO + tt), POOL_WIDTH)
    pos = pos0 + j * tt + lax.broadcasted_iota(jnp.int32, (bt, tt, POOL_GROUP), 1)
    acc, win, means = ext, 1, []
    for gi, target in enumerate(POOL_WINDOWS):
        while win < target:
            acc = acc + pltpu.roll(acc, win, 0)
            win *= 2
        tot = acc.reshape(bt, POOL_HALO + tt, POOL_WIDTH)[:, POOL_HALO:, gi * POOL_GROUP:(gi + 1) * POOL_GROUP]
        means.append(tot / jnp.minimum(pos + 1, target).astype(_F32))
    pooled = (jnp.concatenate(means, axis=-1) - pp).reshape(m, POOL_WIDTH).astype(_BF16)
    z = jnp.concatenate(
        [_dot(pooled[:, g * POOL_GROUP:(g + 1) * POOL_GROUP], pw_ref[g]) for g in range(N_POOL_GROUPS)],
        axis=-1) * psc_ref[...]
    a = _bdot(or_ref[...].reshape(m, MIX_WIDTH), wr_ref[...])
    b = _bdot(z, wp_ref[...])
    merged = (jax.nn.sigmoid(ppg[:, POOL_WIDTH:POOL_WIDTH + d]) * a
              + jax.nn.sigmoid(ppg[:, POOL_WIDTH + d:]) * b)
    o_ref[...] = x_ref[...] + gt_ref[...] * _bdot(merged, wo_ref[...]).reshape(bt, tt, d)


def _merge(x, sc, sh, gt, g, o_r, pool_prev, pos0, w_in_pg, pool_w, pool_scale, w_br_rwkv, w_br_pool, w_out):
    B, T, D = x.shape
    bt, tt = _tile(B, T)
    row = lambda n: pl.BlockSpec((bt, tt, n), lambda i, j: (i, j, 0))
    mod = pl.BlockSpec((bt, 1, D), lambda i, j: (i, 0, 0))
    full = lambda a: pl.BlockSpec(a.shape, lambda i, j: (0,) * a.ndim)
    halo = pl.BlockSpec((bt, POOL_HALO, POOL_WIDTH), lambda i, j: (i, 0, 0))
    ps = pool_scale.reshape(1, POOL_WIDTH)
    g = g.reshape(1, D)
    ps0 = jnp.pad(pool_prev, ((0, 0), (POOL_HALO - POOL_BUF, 0), (0, 0)))
    x_new, pool_new = pl.pallas_call(
        functools.partial(_merge_kernel, pos0=pos0),
        grid=(B // bt, T // tt),
        in_specs=[row(D), mod, mod, mod, full(g), row(MIX_WIDTH), halo,
                  full(w_in_pg), full(pool_w), full(ps), full(w_br_rwkv), full(w_br_pool), full(w_out)],
        out_specs=[row(D), halo],
        out_shape=[jax.ShapeDtypeStruct((B, T, D), _F32),
                   jax.ShapeDtypeStruct((B, POOL_HALO, POOL_WIDTH), _F32)],
        compiler_params=_cparams("arbitrary", "arbitrary"),
        name="merge",
    )(x, sc, sh, gt, g, o_r, ps0, w_in_pg, pool_w, ps, w_br_rwkv, w_br_pool, w_out)
    return x_new, pool_new[:, POOL_HALO - POOL_BUF:]


def _mlp_kernel(x_ref, sc_ref, sh_ref, gt_ref, g_ref, w1_ref, w2_ref, gf_ref, o_ref, *, final_norm):
    bt, tt, d = x_ref.shape
    x = x_ref[...]
    h = _rms(x, g_ref[...]) * (1.0 + sc_ref[...]) + sh_ref[...]
    hb = h.reshape(bt * tt, d).astype(_BF16)
    acc = jnp.zeros((bt * tt, d), _F32)
    for c in range(D_FF // FF_CHUNK):
        u = jnp.maximum(_dot(hb, w1_ref[:, c * FF_CHUNK:(c + 1) * FF_CHUNK]), 0.0)
        acc = acc + _bdot(u * u, w2_ref[c * FF_CHUNK:(c + 1) * FF_CHUNK, :])
    y = x + gt_ref[...] * acc.reshape(bt, tt, d)
    if final_norm:
        y = _rms(y, gf_ref[...])
    o_ref[...] = y


def _mlp(x, sc, sh, gt, g, w1, w2, g_final, final_norm):
    B, T, D = x.shape
    bt, tt = _tile(B, T)
    row = pl.BlockSpec((bt, tt, D), lambda i, j: (i, j, 0))
    mod = pl.BlockSpec((bt, 1, D), lambda i, j: (i, 0, 0))
    vec = pl.BlockSpec((1, D), lambda i, j: (0, 0))
    return pl.pallas_call(
        functools.partial(_mlp_kernel, final_norm=final_norm),
        grid=(B // bt, T // tt),
        in_specs=[row, mod, mod, mod, vec,
                  pl.BlockSpec((D, D_FF), lambda i, j: (0, 0)),
                  pl.BlockSpec((D_FF, D), lambda i, j: (0, 0)),
                  vec],
        out_specs=row,
        out_shape=jax.ShapeDtypeStruct((B, T, D), _F32),
        compiler_params=_cparams("arbitrary", "arbitrary"),
        name="mlp",
    )(x, sc, sh, gt, g.reshape(1, D), w1, w2, g_final.reshape(1, D))


def _trunk(x, mods_mix, mods_mlp, shift0, pool0, wkv0, pos0, W, mix_tile):
    B = x.shape[0]
    shifts, pools, wkvs = [], [], []
    v_first = None
    for l in range(DEPTH):
        lp = {name: arr[l] for name, arr in W.items() if name not in ("v0", "v1", "v2", "norm_final")}
        if l > 0:
            lp.update({name: W[name][l - 1] for name in ("v0", "v1", "v2")})
        sh, sc, gt = (m.reshape(B, 1, D_MODEL) for m in jnp.split(mods_mix[l], 3, axis=-1))
        pr = _in_proj(x, sc, sh, lp["norm_mix"], lp["w_in_r"])
        o_r, v_first, s_wkv = _time_mix(pr, shift0[l], v_first, wkv0[l], lp, *mix_tile)
        x, s_pool = _merge(x, sc, sh, gt, lp["norm_mix"], o_r, pool0[l], pos0, lp["w_in_pg"], lp["pool_w"],
                           lp["pool_scale"], lp["w_br_rwkv"], lp["w_br_pool"], lp["w_out"])
        sh, sc, gt = (m.reshape(B, 1, D_MODEL) for m in jnp.split(mods_mlp[l], 3, axis=-1))
        x = _mlp(x, sc, sh, gt, lp["norm_mlp"], lp["w_ff1"], lp["w_ff2"], W["norm_final"],
                 final_norm=(l == DEPTH - 1))
        shifts.append(pr[:, -1])
        pools.append(s_pool)
        wkvs.append(s_wkv)
    return x, jnp.stack(shifts), jnp.stack(pools), jnp.stack(wkvs)


def kernel(x_prompt, x_sample, state_shift, state_pool, state_wkv, c_prompt, c_sample, w_ada_mix, b_ada_mix, norm_mix, w_in, mu_shift, w0, w2, a0, a2, g2, v0, v1, v2, k_k, k_a, r_k, ln_w, ln_b, pool_w, pool_scale, w_br_rwkv, w_br_pool, w_out, w_ada_mlp, b_ada_mlp, norm_mlp, w_ff1, w_ff2, norm_final):
    bf = lambda a: a.astype(_BF16)
    W = {
        "norm_mix": norm_mix, "w_in_r": bf(w_in[:, :, :RWKV_COLS]), "w_in_pg": bf(w_in[:, :, RWKV_COLS:]),
        "mu_shift": mu_shift, "w0": w0, "w2": w2, "a0": a0,
        "a2": a2, "g2": g2, "v0": v0, "v1": v1, "v2": v2, "k_k": k_k, "k_a": k_a,
        "r_k": r_k, "ln_w": ln_w, "ln_b": ln_b, "pool_w": bf(pool_w), "pool_scale": pool_scale,
        "w_br_rwkv": bf(w_br_rwkv), "w_br_pool": bf(w_br_pool), "w_out": bf(w_out),
        "norm_mlp": norm_mlp, "w_ff1": bf(w_ff1), "w_ff2": bf(w_ff2), "norm_final": norm_final,
    }
    Bp = x_prompt.shape[0]
    c_all = jnp.concatenate([c_prompt, c_sample], axis=0)
    mods_mix = _ada_mod(c_all, w_ada_mix, b_ada_mix)
    mods_mlp = _ada_mod(c_all, w_ada_mlp, b_ada_mlp)
    shift0 = jnp.zeros((DEPTH, Bp, RWKV_COLS), _F32)
    pool0 = jnp.zeros((DEPTH, Bp, POOL_BUF, POOL_WIDTH), _F32)
    wkv0 = jnp.zeros((DEPTH, Bp, N_HEADS, HEAD_SIZE, HEAD_SIZE), _F32)
    y_p, shift_p, pool_p, wkv_p = _trunk(x_prompt, mods_mix[:, :Bp], mods_mlp[:, :Bp],
                                         shift0, pool0, wkv0, 0, W, mix_tile=(PROMPT_SEQS_PER_TILE, WKV_CHUNK))
    y_s, shift_s, pool_s, wkv_s = _trunk(x_sample, mods_mix[:, Bp:], mods_mlp[:, Bp:],
                                         state_shift, state_pool, state_wkv, PAST_LEN, W,
                                         mix_tile=(SAMPLE_SEQS_PER_TILE, x_sample.shape[1]))
    return (y_p, y_s, shift_p, pool_p, wkv_p, shift_s, pool_s, wkv_s)
```

```python
import functools

import jax
import jax.numpy as jnp
from jax import lax
from jax.experimental import pallas as pl
from jax.experimental.pallas import tpu as pltpu

D_MODEL = 1024
DEPTH = 4
PAST_LEN = 16384
MIX_WIDTH = D_MODEL // 2
HEAD_SIZE = 64
N_HEADS = MIX_WIDTH // HEAD_SIZE
D_DECAY_LORA = 64
D_AAA_LORA = 64
D_GATE_LORA = 128
POOL_WIDTH = D_MODEL // 2
POOL_WINDOWS = (2, 4, 8, 16)
N_POOL_GROUPS = len(POOL_WINDOWS)
POOL_GROUP = POOL_WIDTH // N_POOL_GROUPS
POOL_BUF = max(POOL_WINDOWS) - 1
POOL_HALO = max(POOL_WINDOWS)
D_FF = 4 * D_MODEL
RWKV_COLS = 3 * MIX_WIDTH + D_DECAY_LORA + D_AAA_LORA + D_GATE_LORA
GATE_COLS = 2 * D_MODEL
RWKV_SPLITS = (MIX_WIDTH, 2 * MIX_WIDTH, 3 * MIX_WIDTH,
               3 * MIX_WIDTH + D_DECAY_LORA, 3 * MIX_WIDTH + D_DECAY_LORA + D_AAA_LORA)
NORM_EPS = 1e-6
GN_EPS = 64e-5
LOG_DECAY_SCALE = -0.6065306597126334

V7X_VMEM_LIMIT_BYTES = 56 * 1024 * 1024
ROW_TILE = 512
FF_CHUNK = 1024
WKV_CHUNK = 64
WKV_SOLVE_BLOCK = 16
SAMPLE_SEQS_PER_TILE = 8
PROMPT_SEQS_PER_TILE = 4

_BF16 = jnp.bfloat16
_F32 = jnp.float32


def _cparams(*sem):
    return pltpu.CompilerParams(dimension_semantics=sem, vmem_limit_bytes=V7X_VMEM_LIMIT_BYTES)


def _dot(a, b):
    return jnp.dot(a, b, preferred_element_type=_F32)


def _bdot(a, b):
    return _dot(a.astype(_BF16), b.astype(_BF16))


def _bmm(a, b, contract=(2, 1)):
    dims = (((contract[0],), (contract[1],)), ((0,), (0,)))
    return lax.dot_general(a.astype(_BF16), b.astype(_BF16), dims, preferred_element_type=_F32)


def _split2(x):
    hi = x.astype(_BF16)
    return hi, (x - hi.astype(_F32)).astype(_BF16)


def _split3(x):
    hi, _ = _split2(x)
    r1 = x - hi.astype(_F32)
    mid, lo = _split2(r1)
    return hi, mid, lo


def _rms(x, g):
    return x * lax.rsqrt(jnp.mean(x * x, axis=-1, keepdims=True) + NORM_EPS) * g


def _ada_norm(x_ref, g_ref, sc_ref, sh_ref):
    bt, tt, d = x_ref.shape
    h = _rms(x_ref[...], g_ref[...]) * (1.0 + sc_ref[...]) + sh_ref[...]
    return h.reshape(bt * tt, d).astype(_BF16)


def _tile(B, T):
    tt = min(T, ROW_TILE)
    return ROW_TILE // tt, tt


def _ada_kernel(c_ref, w_ref, b_ref, o_ref):
    c = c_ref[...]
    o_ref[0] = _bdot(c * jax.nn.sigmoid(c), w_ref[0]) + b_ref[0]


def _ada_mod(c, w, b, tn=1024):
    L, D, N = w.shape
    M = c.shape[0]
    return pl.pallas_call(
        _ada_kernel,
        grid=(L, N // tn),
        in_specs=[
            pl.BlockSpec((M, D), lambda l, j: (0, 0)),
            pl.BlockSpec((1, D, tn), lambda l, j: (l, 0, j)),
            pl.BlockSpec((1, 1, tn), lambda l, j: (l, 0, j)),
        ],
        out_specs=pl.BlockSpec((1, M, tn), lambda l, j: (l, 0, j)),
        out_shape=jax.ShapeDtypeStruct((L, M, N), _F32),
        compiler_params=_cparams("arbitrary", "arbitrary"),
        name="ada_mod",
    )(c, w, b.reshape(L, 1, N))


def _wkv_chunk_math(kkp, rp, bn, kn, bnc, knc, v, g_last, s_prev):
    G, C, N = v.shape
    blk_size = min(WKV_SOLVE_BLOCK, C)
    ti = lax.broadcasted_iota(jnp.int32, (G, C, C), 1)
    si = lax.broadcasted_iota(jnp.int32, (G, C, C), 2)
    eye = (ti == si).astype(_F32)
    strict = ti > si
    blk = (ti // blk_size) == (si // blk_size)
    t2 = lax.broadcasted_iota(jnp.int32, (G, C, 2 * C), 1)
    s2 = lax.broadcasted_iota(jnp.int32, (G, C, 2 * C), 2)
    mask_kk = (s2 >= C) & (t2 > s2 - C)
    mask_y = ((s2 < C) & (t2 >= s2)) | ((s2 >= C) & (t2 >= s2 - C))

    lhs = jnp.concatenate([kkp, rp], axis=1)
    rhs = jnp.concatenate([bn, kn], axis=1)
    a = _bmm(lhs, rhs, (2, 2))
    w1 = _bmm(lhs, s_prev, (2, 2))
    rhs_u = w1[:, :C] + _bmm(jnp.where(mask_kk, a[:, :C], 0.0), jnp.concatenate([v, v], axis=1))
    a_kb = jnp.where(strict, a[:, :C, :C], 0.0)
    d = jnp.where(blk, a_kb, 0.0)
    tm = eye - d
    p = d
    n = 1
    while 2 * n < blk_size:
        p = _bmm(p, p)
        tm = _bmm(tm, eye + p)
        n *= 2
    x = _bmm(tm, rhs_u)
    if C > blk_size:
        mo = _bmm(tm, a_kb - d)
        x = x - _bmm(mo, x)
        p = mo
        n = 1
        while 2 * n < C // blk_size:
            p = _bmm(p, p)
            x = x + _bmm(p, x)
            n *= 2
    nuv = jnp.concatenate([-x, v], axis=1)
    y = w1[:, C:] + _bmm(jnp.where(mask_y, a[:, C:], 0.0), nuv)
    s_new = s_prev * g_last + _bmm(nuv, jnp.concatenate([bnc, knc], axis=1), (1, 1))
    return y, s_new


def _mix_kernel(*refs, has_vfirst):
    it = iter(refs)
    x_ref, sc_ref, sh_ref, g_ref, wi_ref, sh0_ref = (next(it) for _ in range(6))
    vf_ref = next(it) if has_vfirst else None
    s0_ref, ones_ref, mu_ref, w0_ref, w2_ref, a0_ref, a2_ref, g2_ref = (next(it) for _ in range(8))
    v0_ref, v1_ref, v2_ref = (next(it) for _ in range(3)) if has_vfirst else (None, None, None)
    kk_ref, ka_ref, rk_ref, lnw_ref, lnb_ref = (next(it) for _ in range(5))
    o_ref = next(it)
    vfo_ref = None if has_vfirst else next(it)
    s_ref, last_ref = next(it), next(it)

    @pl.when(pl.program_id(1) == 0)
    def _():
        last_ref[...] = sh0_ref[...]
        s_ref[...] = s0_ref[...]

    n_seq, L, _ = x_ref.shape
    rows = n_seq * L
    H, N, M = N_HEADS, HEAD_SIZE, MIX_WIDTH
    x = _dot(_ada_norm(x_ref, g_ref, sc_ref, sh_ref), wi_ref[...])
    init = jnp.broadcast_to(last_ref[...], (n_seq, L, RWKV_COLS)).reshape(rows, RWKV_COLS)
    first = (lax.broadcasted_iota(jnp.int32, (rows, RWKV_COLS), 0) & (L - 1)) == 0
    prev = jnp.where(first, init, pltpu.roll(x, 1, 0))
    last_ref[...] = x.reshape(n_seq, L, RWKV_COLS)[:, L - 1:L, :]
    xl = x + (prev - x) * mu_ref[...]
    r, k, v = xl[:, :M], xl[:, M:2 * M], xl[:, 2 * M:3 * M]
    xw, xa, xg = (xl[:, RWKV_SPLITS[2]:RWKV_SPLITS[3]], xl[:, RWKV_SPLITS[3]:RWKV_SPLITS[4]],
                  xl[:, RWKV_SPLITS[4]:])
    lw = LOG_DECAY_SCALE * jax.nn.sigmoid(w0_ref[...] + _bdot(jnp.tanh(xw), w2_ref[...]))
    a = jax.nn.sigmoid(a0_ref[...] + _bdot(xa, a2_ref[...]))
    g = _bdot(jax.nn.sigmoid(xg), g2_ref[...])
    if has_vfirst:
        vf = vf_ref[...].reshape(rows, M)
        v = v + (vf - v) * jax.nn.sigmoid(v0_ref[...] + _bdot(_bdot(v, v1_ref[...]), v2_ref[...]))
    else:
        vfo_ref[...] = v.reshape(n_seq, L, M)

    def head_sum(t):
        return sum(_dot(p, ones_ref[...]) for p in _split2(t))

    kk = k * kk_ref[...]
    kk = kk * lax.rsqrt(head_sum(kk * kk) + 1e-12)
    k = k * (1.0 + (a - 1.0) * ka_ref[...])
    b = kk * a
    lw3 = lw.reshape(n_seq, L, M)
    ti = lax.broadcasted_iota(jnp.int32, (n_seq, L, L), 1)
    si = lax.broadcasted_iota(jnp.int32, (n_seq, L, L), 2)
    tril_incl = (ti >= si).astype(_BF16)
    cl3 = sum(_bmm(tril_incl, p) for p in _split3(lw3))
    cl_last3 = cl3[:, L - 1:L, :]
    cl = cl3.reshape(rows, M)
    e_neg = jnp.exp(-cl)
    e_out = jnp.exp(cl_last3 - cl3).reshape(rows, M)

    def split_heads(t):
        t3 = t.reshape(n_seq, -1, M)
        return jnp.concatenate([t3[:, :, h * N:(h + 1) * N] for h in range(H)], axis=0)

    def per_chain(ref):
        return jnp.broadcast_to(ref[...][:, None], (H, n_seq, 1, N)).reshape(H * n_seq, 1, N)

    operands = (kk * jnp.exp(cl - lw), r * jnp.exp(cl), b * e_neg, k * e_neg, b * e_out, k * e_out, v)
    vh = split_heads(v)
    y, s_new = _wkv_chunk_math(*(split_heads(t) for t in operands[:-1]), vh,
                               split_heads(jnp.exp(cl_last3).reshape(n_seq, M)),
                               s_ref[...].reshape(H * n_seq, N, N))
    s_ref[...] = s_new.reshape(H, n_seq, N, N)
    yc = y - jnp.mean(y, axis=-1, keepdims=True)
    var = jnp.mean(yc * yc, axis=-1, keepdims=True)
    yn = yc * lax.rsqrt(var + GN_EPS) * per_chain(lnw_ref) + per_chain(lnb_ref)
    bonus = jnp.sum(split_heads(r * k * rk_ref[...]), axis=-1, keepdims=True) * vh
    o = (yn + bonus) * split_heads(g)
    o_ref[...] = jnp.concatenate([o[h * n_seq:(h + 1) * n_seq] for h in range(H)], axis=-1)


def _time_mix(x, sc, sh, shift_prev, v_first, s0, lp, n_seq, seq_len):
    B, T, D = x.shape
    H, N, M = N_HEADS, HEAD_SIZE, MIX_WIDTH
    has_vfirst = v_first is not None
    row = lambda n: pl.BlockSpec((n_seq, seq_len, n), lambda i, j: (i, j, 0))
    per_seq = lambda n: pl.BlockSpec((n_seq, 1, n), lambda i, j: (i, 0, 0))
    full = lambda a: pl.BlockSpec(a.shape, lambda i, j: (0,) * a.ndim)
    state = pl.BlockSpec((H, n_seq, N, N), lambda i, j: (0, i, 0, 0))
    vec = lambda a: a.reshape(1, -1)
    hp = lambda a: a.reshape(H, 1, N)
    ones_bd = jnp.kron(jnp.eye(H, dtype=_F32), jnp.ones((N, N), _F32)).astype(_BF16)
    params = [ones_bd, vec(lp["mu_shift"]), vec(lp["w0"]), lp["w2"], vec(lp["a0"]), lp["a2"], lp["g2"]]
    if has_vfirst:
        params += [vec(lp["v0"]), lp["v1"], lp["v2"]]
    params += [vec(lp["k_k"]), vec(lp["k_a"]), vec(lp["r_k"]), hp(lp["ln_w"]), hp(lp["ln_b"])]
    front = [x, sc, sh, vec(lp["norm_mix"]), lp["w_in_r"], shift_prev.reshape(B, 1, RWKV_COLS)]
    args = front + ([v_first] if has_vfirst else []) + [s0.transpose(1, 0, 2, 3)] + params
    in_specs = [row(D), per_seq(D), per_seq(D), full(front[3]), full(front[4]), per_seq(RWKV_COLS)]
    in_specs += ([row(M)] if has_vfirst else []) + [state] + [full(p) for p in params]
    out_specs = [row(M)] + ([] if has_vfirst else [row(M)]) + [state, per_seq(RWKV_COLS)]
    out_shape = [jax.ShapeDtypeStruct((B, T, M), _F32)] * (1 if has_vfirst else 2)
    out_shape += [jax.ShapeDtypeStruct((H, B, N, N), _F32), jax.ShapeDtypeStruct((B, 1, RWKV_COLS), _F32)]
    outs = pl.pallas_call(
        functools.partial(_mix_kernel, has_vfirst=has_vfirst),
        grid=(B // n_seq, T // seq_len),
        in_specs=in_specs,
        out_specs=out_specs,
        out_shape=out_shape,
        compiler_params=_cparams("arbitrary", "arbitrary"),
        name="time_mix",
    )(*args)
    v_first = v_first if has_vfirst else outs[1]
    return outs[0], v_first, outs[-2].transpose(1, 0, 2, 3), outs[-1].reshape(B, RWKV_COLS)


def _merge_kernel(x_ref, sc_ref, sh_ref, gt_ref, g_ref, or_ref, ps0_ref, wi_ref, pw_ref, psc_ref, wr_ref,
                  wp_ref, wo_ref, o_ref, ps_ref, *, pos0):
    bt, tt, d = x_ref.shape
    m = bt * tt
    j = pl.program_id(1)

    @pl.when(j == 0)
    def _():
        ps_ref[...] = ps0_ref[...]

    ppg = _dot(_ada_norm(x_ref, g_ref, sc_ref, sh_ref), wi_ref[...])
    pp = ppg[:, :POOL_WIDTH].reshape(bt, tt, POOL_WIDTH)
    ext3 = jnp.concatenate([ps_ref[...], pp], axis=1)
    ps_ref[...] = ext3[:, tt:tt + POOL_HALO, :]
    ext = ext3.reshape(bt * (POOL_HALO + tt), POOL_WIDTH)
    pos = pos0 + j * tt + lax.broadcasted_iota(jnp.int32, (bt, tt, POOL_GROUP), 1)
    acc, win, means = ext, 1, []
    for gi, target in enumerate(POOL_WINDOWS):
        while win < target:
            acc = acc + pltpu.roll(acc, win, 0)
            win *= 2
        tot = acc.reshape(bt, POOL_HALO + tt, POOL_WIDTH)[:, POOL_HALO:, gi * POOL_GROUP:(gi + 1) * POOL_GROUP]
        means.append(tot / jnp.minimum(pos + 1, target).astype(_F32))
    pooled = (jnp.concatenate(means, axis=-1) - pp).reshape(m, POOL_WIDTH).astype(_BF16)
    z = jnp.concatenate(
        [_dot(pooled[:, g * POOL_GROUP:(g + 1) * POOL_GROUP], pw_ref[g]) for g in range(N_POOL_GROUPS)],
        axis=-1) * psc_ref[...]
    a = _bdot(or_ref[...].reshape(m, MIX_WIDTH), wr_ref[...])
    b = _bdot(z, wp_ref[...])
    merged = (jax.nn.sigmoid(ppg[:, POOL_WIDTH:POOL_WIDTH + d]) * a
              + jax.nn.sigmoid(ppg[:, POOL_WIDTH + d:]) * b)
    o_ref[...] = x_ref[...] + gt_ref[...] * _bdot(merged, wo_ref[...]).reshape(bt, tt, d)


def _merge(x, sc, sh, gt, g, o_r, pool_prev, pos0, w_in_pg, pool_w, pool_scale, w_br_rwkv, w_br_pool, w_out):
    B, T, D = x.shape
    bt, tt = _tile(B, T)
    row = lambda n: pl.BlockSpec((bt, tt, n), lambda i, j: (i, j, 0))
    mod = pl.BlockSpec((bt, 1, D), lambda i, j: (i, 0, 0))
    full = lambda a: pl.BlockSpec(a.shape, lambda i, j: (0,) * a.ndim)
    halo = pl.BlockSpec((bt, POOL_HALO, POOL_WIDTH), lambda i, j: (i, 0, 0))
    ps = pool_scale.reshape(1, POOL_WIDTH)
    g = g.reshape(1, D)
    ps0 = jnp.pad(pool_prev, ((0, 0), (POOL_HALO - POOL_BUF, 0), (0, 0)))
    x_new, pool_new = pl.pallas_call(
        functools.partial(_merge_kernel, pos0=pos0),
        grid=(B // bt, T // tt),
        in_specs=[row(D), mod, mod, mod, full(g), row(MIX_WIDTH), halo,
                  full(w_in_pg), full(pool_w), full(ps), full(w_br_rwkv), full(w_br_pool), full(w_out)],
        out_specs=[row(D), halo],
        out_shape=[jax.ShapeDtypeStruct((B, T, D), _F32),
                   jax.ShapeDtypeStruct((B, POOL_HALO, POOL_WIDTH), _F32)],
        compiler_params=_cparams("arbitrary", "arbitrary"),
        name="merge",
    )(x, sc, sh, gt, g, o_r, ps0, w_in_pg, pool_w, ps, w_br_rwkv, w_br_pool, w_out)
    return x_new, pool_new[:, POOL_HALO - POOL_BUF:]


def _mlp_kernel(x_ref, sc_ref, sh_ref, gt_ref, g_ref, w1_ref, w2_ref, gf_ref, o_ref, *, final_norm):
    bt, tt, d = x_ref.shape
    hb = _ada_norm(x_ref, g_ref, sc_ref, sh_ref)
    acc = jnp.zeros((bt * tt, d), _F32)
    for c in range(D_FF // FF_CHUNK):
        u = jnp.maximum(_dot(hb, w1_ref[:, c * FF_CHUNK:(c + 1) * FF_CHUNK]), 0.0)
        acc = acc + _bdot(u * u, w2_ref[c * FF_CHUNK:(c + 1) * FF_CHUNK, :])
    y = x_ref[...] + gt_ref[...] * acc.reshape(bt, tt, d)
    if final_norm:
        y = _rms(y, gf_ref[...])
    o_ref[...] = y


def _mlp(x, sc, sh, gt, g, w1, w2, g_final, final_norm):
    B, T, D = x.shape
    bt, tt = _tile(B, T)
    row = pl.BlockSpec((bt, tt, D), lambda i, j: (i, j, 0))
    mod = pl.BlockSpec((bt, 1, D), lambda i, j: (i, 0, 0))
    vec = pl.BlockSpec((1, D), lambda i, j: (0, 0))
    return pl.pallas_call(
        functools.partial(_mlp_kernel, final_norm=final_norm),
        grid=(B // bt, T // tt),
        in_specs=[row, mod, mod, mod, vec,
                  pl.BlockSpec((D, D_FF), lambda i, j: (0, 0)),
                  pl.BlockSpec((D_FF, D), lambda i, j: (0, 0)),
                  vec],
        out_specs=row,
        out_shape=jax.ShapeDtypeStruct((B, T, D), _F32),
        compiler_params=_cparams("arbitrary", "arbitrary"),
        name="mlp",
    )(x, sc, sh, gt, g.reshape(1, D), w1, w2, g_final.reshape(1, D))


def _trunk(x, mods_mix, mods_mlp, shift0, pool0, wkv0, pos0, W, mix_tile):
    B = x.shape[0]
    shifts, pools, wkvs = [], [], []
    v_first = None
    for l in range(DEPTH):
        lp = {name: arr[l] for name, arr in W.items() if name not in ("v0", "v1", "v2", "norm_final")}
        if l > 0:
            lp.update({name: W[name][l - 1] for name in ("v0", "v1", "v2")})
        sh, sc, gt = (m.reshape(B, 1, D_MODEL) for m in jnp.split(mods_mix[l], 3, axis=-1))
        o_r, v_first, s_wkv, s_shift = _time_mix(x, sc, sh, shift0[l], v_first, wkv0[l], lp, *mix_tile)
        x, s_pool = _merge(x, sc, sh, gt, lp["norm_mix"], o_r, pool0[l], pos0, lp["w_in_pg"], lp["pool_w"],
                           lp["pool_scale"], lp["w_br_rwkv"], lp["w_br_pool"], lp["w_out"])
        sh, sc, gt = (m.reshape(B, 1, D_MODEL) for m in jnp.split(mods_mlp[l], 3, axis=-1))
        x = _mlp(x, sc, sh, gt, lp["norm_mlp"], lp["w_ff1"], lp["w_ff2"], W["norm_final"],
                 final_norm=(l == DEPTH - 1))
        shifts.append(s_shift)
        pools.append(s_pool)
        wkvs.append(s_wkv)
    return x, jnp.stack(shifts), jnp.stack(pools), jnp.stack(wkvs)


def kernel(x_prompt, x_sample, state_shift, state_pool, state_wkv, c_prompt, c_sample, w_ada_mix, b_ada_mix, norm_mix, w_in, mu_shift, w0, w2, a0, a2, g2, v0, v1, v2, k_k, k_a, r_k, ln_w, ln_b, pool_w, pool_scale, w_br_rwkv, w_br_pool, w_out, w_ada_mlp, b_ada_mlp, norm_mlp, w_ff1, w_ff2, norm_final):
    bf = lambda a: a.astype(_BF16)
    W = {
        "norm_mix": norm_mix, "w_in_r": bf(w_in[:, :, :RWKV_COLS]), "w_in_pg": bf(w_in[:, :, RWKV_COLS:]),
        "mu_shift": mu_shift, "w0": w0, "w2": w2, "a0": a0,
        "a2": a2, "g2": g2, "v0": v0, "v1": v1, "v2": v2, "k_k": k_k, "k_a": k_a,
        "r_k": r_k, "ln_w": ln_w, "ln_b": ln_b, "pool_w": bf(pool_w), "pool_scale": pool_scale,
        "w_br_rwkv": bf(w_br_rwkv), "w_br_pool": bf(w_br_pool), "w_out": bf(w_out),
        "norm_mlp": norm_mlp, "w_ff1": bf(w_ff1), "w_ff2": bf(w_ff2), "norm_final": norm_final,
    }
    Bp = x_prompt.shape[0]
    c_all = jnp.concatenate([c_prompt, c_sample], axis=0)
    mods_mix = _ada_mod(c_all, w_ada_mix, b_ada_mix)
    mods_mlp = _ada_mod(c_all, w_ada_mlp, b_ada_mlp)
    shift0 = jnp.zeros((DEPTH, Bp, RWKV_COLS), _F32)
    pool0 = jnp.zeros((DEPTH, Bp, POOL_BUF, POOL_WIDTH), _F32)
    wkv0 = jnp.zeros((DEPTH, Bp, N_HEADS, HEAD_SIZE, HEAD_SIZE), _F32)
    y_p, shift_p, pool_p, wkv_p = _trunk(x_prompt, mods_mix[:, :Bp], mods_mlp[:, :Bp],
                                         shift0, pool0, wkv0, 0, W, mix_tile=(PROMPT_SEQS_PER_TILE, WKV_CHUNK))
    y_s, shift_s, pool_s, wkv_s = _trunk(x_sample, mods_mix[:, Bp:], mods_mlp[:, Bp:],
                                         state_shift, state_pool, state_wkv, PAST_LEN, W,
                                         mix_tile=(SAMPLE_SEQS_PER_TILE, x_sample.shape[1]))
    return (y_p, y_s, shift_p, pool_p, wkv_p, shift_s, pool_s, wkv_s)
```

```python
import functools

import jax
import jax.numpy as jnp
from jax import lax
from jax.experimental import pallas as pl
from jax.experimental.pallas import tpu as pltpu

D_MODEL = 1024
DEPTH = 4
PAST_LEN = 16384
MIX_WIDTH = D_MODEL // 2
HEAD_SIZE = 64
N_HEADS = MIX_WIDTH // HEAD_SIZE
D_DECAY_LORA = 64
D_AAA_LORA = 64
D_GATE_LORA = 128
POOL_WIDTH = D_MODEL // 2
POOL_WINDOWS = (2, 4, 8, 16)
N_POOL_GROUPS = len(POOL_WINDOWS)
POOL_GROUP = POOL_WIDTH // N_POOL_GROUPS
POOL_BUF = max(POOL_WINDOWS) - 1
POOL_HALO = max(POOL_WINDOWS)
D_FF = 4 * D_MODEL
RWKV_COLS = 3 * MIX_WIDTH + D_DECAY_LORA + D_AAA_LORA + D_GATE_LORA
GATE_COLS = 2 * D_MODEL
RWKV_SPLITS = (MIX_WIDTH, 2 * MIX_WIDTH, 3 * MIX_WIDTH,
               3 * MIX_WIDTH + D_DECAY_LORA, 3 * MIX_WIDTH + D_DECAY_LORA + D_AAA_LORA)
NORM_EPS = 1e-6
GN_EPS = 64e-5
LOG_DECAY_SCALE = -0.6065306597126334

V7X_VMEM_LIMIT_BYTES = 56 * 1024 * 1024
ROW_TILE = 512
FF_CHUNK = 1024
WKV_CHUNK = 64
WKV_SOLVE_BLOCK = 16
SAMPLE_SEQS_PER_TILE = 8
PROMPT_SEQS_PER_TILE = 4
PROMPT_SEQS_PER_GROUP = 4

_BF16 = jnp.bfloat16
_F32 = jnp.float32


def _cparams(*sem):
    return pltpu.CompilerParams(dimension_semantics=sem, vmem_limit_bytes=V7X_VMEM_LIMIT_BYTES)


def _dot(a, b):
    return jnp.dot(a, b, preferred_element_type=_F32)


def _bdot(a, b):
    return _dot(a.astype(_BF16), b.astype(_BF16))


def _bmm(a, b, contract=(2, 1)):
    dims = (((contract[0],), (contract[1],)), ((0,), (0,)))
    return lax.dot_general(a.astype(_BF16), b.astype(_BF16), dims, preferred_element_type=_F32)


def _split2(x):
    hi = x.astype(_BF16)
    return hi, (x - hi.astype(_F32)).astype(_BF16)


def _split3(x):
    hi, _ = _split2(x)
    r1 = x - hi.astype(_F32)
    mid, lo = _split2(r1)
    return hi, mid, lo


def _rms(x, g):
    return x * lax.rsqrt(jnp.mean(x * x, axis=-1, keepdims=True) + NORM_EPS) * g


def _ada_norm(x, g, sc, sh):
    bt, tt, d = x.shape
    h = _rms(x, g) * (1.0 + sc) + sh
    return h.reshape(bt * tt, d).astype(_BF16)


def _layer(a, l):
    return pl.BlockSpec((None,) + a.shape[1:], lambda i, j: (l,) + (0,) * (a.ndim - 1))


def _tile(B, T):
    tt = min(T, ROW_TILE)
    return ROW_TILE // tt, tt


def _ada_kernel(c_ref, w_ref, b_ref, o_ref):
    c = c_ref[...]
    o_ref[0] = _bdot(c * jax.nn.sigmoid(c), w_ref[0]) + b_ref[0]


def _ada_mod(c, w, b, tn=1024):
    L, D, N = w.shape
    M = c.shape[0]
    return pl.pallas_call(
        _ada_kernel,
        grid=(L, N // tn),
        in_specs=[
            pl.BlockSpec((M, D), lambda l, j: (0, 0)),
            pl.BlockSpec((1, D, tn), lambda l, j: (l, 0, j)),
            pl.BlockSpec((1, 1, tn), lambda l, j: (l, 0, j)),
        ],
        out_specs=pl.BlockSpec((1, M, tn), lambda l, j: (l, 0, j)),
        out_shape=jax.ShapeDtypeStruct((L, M, N), _F32),
        compiler_params=_cparams("arbitrary", "arbitrary"),
        name="ada_mod",
    )(c, w, b.reshape(L, 1, N))


def _wkv_chunk_math(kkp, rp, bn, kn, bnc, knc, v, g_last, s_prev):
    G, C, N = v.shape
    blk_size = min(WKV_SOLVE_BLOCK, C)
    ti = lax.broadcasted_iota(jnp.int32, (G, C, C), 1)
    si = lax.broadcasted_iota(jnp.int32, (G, C, C), 2)
    eye = (ti == si).astype(_F32)
    strict = ti > si
    blk = (ti // blk_size) == (si // blk_size)
    t2 = lax.broadcasted_iota(jnp.int32, (G, C, 2 * C), 1)
    s2 = lax.broadcasted_iota(jnp.int32, (G, C, 2 * C), 2)
    mask_kk = (s2 >= C) & (t2 > s2 - C)
    mask_y = ((s2 < C) & (t2 >= s2)) | ((s2 >= C) & (t2 >= s2 - C))

    lhs = jnp.concatenate([kkp, rp], axis=1)
    rhs = jnp.concatenate([bn, kn], axis=1)
    a = _bmm(lhs, rhs, (2, 2))
    w1 = _bmm(lhs, s_prev, (2, 2))
    rhs_u = w1[:, :C] + _bmm(jnp.where(mask_kk, a[:, :C], 0.0), jnp.concatenate([v, v], axis=1))
    a_kb = jnp.where(strict, a[:, :C, :C], 0.0)
    d = jnp.where(blk, a_kb, 0.0)
    tm = eye - d
    p = d
    n = 1
    while 2 * n < blk_size:
        p = _bmm(p, p)
        tm = _bmm(tm, eye + p)
        n *= 2
    x = _bmm(tm, rhs_u)
    if C > blk_size:
        mo = _bmm(tm, a_kb - d)
        x = x - _bmm(mo, x)
        p = mo
        n = 1
        while 2 * n < C // blk_size:
            p = _bmm(p, p)
            x = x + _bmm(p, x)
            n *= 2
    nuv = jnp.concatenate([-x, v], axis=1)
    y = w1[:, C:] + _bmm(jnp.where(mask_y, a[:, C:], 0.0), nuv)
    s_new = s_prev * g_last + _bmm(nuv, jnp.concatenate([bnc, knc], axis=1), (1, 1))
    return y, s_new


def _mix_kernel(*refs, has_vfirst, group):
    it = iter(refs)
    x_ref, sc_ref, sh_ref, g_ref, wi_ref, sh0_ref = (next(it) for _ in range(6))
    vf_ref = next(it) if has_vfirst else None
    s0_ref, ones_ref, mu_ref, w0_ref, w2_ref, a0_ref, a2_ref, g2_ref = (next(it) for _ in range(8))
    v0_ref, v1_ref, v2_ref = (next(it) for _ in range(3)) if has_vfirst else (None, None, None)
    kk_ref, ka_ref, rk_ref, lnw_ref, lnb_ref = (next(it) for _ in range(5))
    o_ref = next(it)
    vfo_ref = None if has_vfirst else next(it)
    s_ref, last_ref = next(it), next(it)

    @pl.when(pl.program_id(1) == 0)
    def _():
        last_ref[...] = sh0_ref[...]
        s_ref[...] = s0_ref[...]

    L = x_ref.shape[1]
    n_seq = group
    rows = n_seq * L
    H, N, M = N_HEADS, HEAD_SIZE, MIX_WIDTH

    def head_sum(t):
        return sum(_dot(p, ones_ref[...]) for p in _split2(t))

    def split_heads(t):
        t3 = t.reshape(n_seq, -1, M)
        return jnp.concatenate([t3[:, :, h * N:(h + 1) * N] for h in range(H)], axis=0)

    def per_chain(ref):
        return jnp.broadcast_to(ref[...][:, None], (H, n_seq, 1, N)).reshape(H * n_seq, 1, N)

    for lo in range(0, x_ref.shape[0], group):
        sl = slice(lo, lo + group)
        x = _dot(_ada_norm(x_ref[sl], g_ref[...], sc_ref[sl], sh_ref[sl]), wi_ref[...])
        init = jnp.broadcast_to(last_ref[sl], (n_seq, L, RWKV_COLS)).reshape(rows, RWKV_COLS)
        first = (lax.broadcasted_iota(jnp.int32, (rows, RWKV_COLS), 0) & (L - 1)) == 0
        prev = jnp.where(first, init, pltpu.roll(x, 1, 0))
        last_ref[sl] = x.reshape(n_seq, L, RWKV_COLS)[:, L - 1:L, :]
        xl = x + (prev - x) * mu_ref[...]
        r, k, v = xl[:, :M], xl[:, M:2 * M], xl[:, 2 * M:3 * M]
        xw, xa, xg = (xl[:, RWKV_SPLITS[2]:RWKV_SPLITS[3]], xl[:, RWKV_SPLITS[3]:RWKV_SPLITS[4]],
                      xl[:, RWKV_SPLITS[4]:])
        lw = LOG_DECAY_SCALE * jax.nn.sigmoid(w0_ref[...] + _bdot(jnp.tanh(xw), w2_ref[...]))
        a = jax.nn.sigmoid(a0_ref[...] + _bdot(xa, a2_ref[...]))
        g = _bdot(jax.nn.sigmoid(xg), g2_ref[...])
        if has_vfirst:
            vf = vf_ref[sl].reshape(rows, M)
            v = v + (vf - v) * jax.nn.sigmoid(v0_ref[...] + _bdot(_bdot(v, v1_ref[...]), v2_ref[...]))
        else:
            vfo_ref[sl] = v.reshape(n_seq, L, M)
        kk = k * kk_ref[...]
        kk = kk * lax.rsqrt(head_sum(kk * kk) + 1e-12)
        k = k * (1.0 + (a - 1.0) * ka_ref[...])
        b = kk * a
        lw3 = lw.reshape(n_seq, L, M)
        ti = lax.broadcasted_iota(jnp.int32, (n_seq, L, L), 1)
        si = lax.broadcasted_iota(jnp.int32, (n_seq, L, L), 2)
        tril_incl = (ti >= si).astype(_BF16)
        cl3 = sum(_bmm(tril_incl, p) for p in _split3(lw3))
        cl_last3 = cl3[:, L - 1:L, :]
        cl = cl3.reshape(rows, M)
        e_neg = jnp.exp(-cl)
        e_out = jnp.exp(cl_last3 - cl3).reshape(rows, M)
        operands = (kk * jnp.exp(cl - lw), r * jnp.exp(cl), b * e_neg, k * e_neg, b * e_out, k * e_out)
        vh = split_heads(v)
        y, s_new = _wkv_chunk_math(*(split_heads(t) for t in operands), vh,
                                   split_heads(jnp.exp(cl_last3).reshape(n_seq, M)),
                                   jnp.concatenate([s_ref[sl, h] for h in range(H)], axis=0))
        for h in range(H):
            s_ref[sl, h] = s_new[h * n_seq:(h + 1) * n_seq]
        yc = y - jnp.mean(y, axis=-1, keepdims=True)
        var = jnp.mean(yc * yc, axis=-1, keepdims=True)
        yn = yc * lax.rsqrt(var + GN_EPS) * per_chain(lnw_ref) + per_chain(lnb_ref)
        bonus = jnp.sum(split_heads(r * k * rk_ref[...]), axis=-1, keepdims=True) * vh
        o = (yn + bonus) * split_heads(g)
        o_ref[sl] = jnp.concatenate([o[h * n_seq:(h + 1) * n_seq] for h in range(H)], axis=-1)


def _time_mix(x, sc, sh, shift_prev, v_first, s0, W, l, n_seq, seq_len, group):
    B, T, D = x.shape
    H, N, M = N_HEADS, HEAD_SIZE, MIX_WIDTH
    has_vfirst = v_first is not None
    row = lambda n: pl.BlockSpec((n_seq, seq_len, n), lambda i, j: (i, j, 0))
    per_seq = lambda n: pl.BlockSpec((n_seq, 1, n), lambda i, j: (i, 0, 0))
    state = pl.BlockSpec((n_seq, H, N, N), lambda i, j: (i, 0, 0, 0))
    ones_bd = jnp.kron(jnp.eye(H, dtype=_F32), jnp.ones((N, N), _F32)).astype(_BF16)
    names = ["mu_shift", "w0", "w2", "a0", "a2", "g2"] + (["v0", "v1", "v2"] if has_vfirst else [])
    names += ["k_k", "k_a", "r_k", "ln_w", "ln_b"]
    layer_of = lambda name: l - 1 if name in ("v0", "v1", "v2") else l
    args = [x, sc, sh, W["norm_mix"], W["w_in_r"], shift_prev]
    in_specs = [row(D), per_seq(D), per_seq(D), _layer(W["norm_mix"], l), _layer(W["w_in_r"], l),
                pl.BlockSpec((None, n_seq, 1, RWKV_COLS), lambda i, j: (l, i, 0, 0))]
    if has_vfirst:
        args.append(v_first)
        in_specs.append(row(M))
    args += [s0, ones_bd] + [W[name] for name in names]
    in_specs += [pl.BlockSpec((None, n_seq, H, N, N), lambda i, j: (l, i, 0, 0, 0)),
                 pl.BlockSpec(ones_bd.shape, lambda i, j: (0, 0))]
    in_specs += [_layer(W[name], layer_of(name)) for name in names]
    out_specs = [row(M)] + ([] if has_vfirst else [row(M)]) + [state, per_seq(RWKV_COLS)]
    out_shape = [jax.ShapeDtypeStruct((B, T, M), _F32)] * (1 if has_vfirst else 2)
    out_shape += [jax.ShapeDtypeStruct((B, H, N, N), _F32), jax.ShapeDtypeStruct((B, 1, RWKV_COLS), _F32)]
    outs = pl.pallas_call(
        functools.partial(_mix_kernel, has_vfirst=has_vfirst, group=group),
        grid=(B // n_seq, T // seq_len),
        in_specs=in_specs,
        out_specs=out_specs,
        out_shape=out_shape,
        compiler_params=_cparams("arbitrary", "arbitrary"),
        name="time_mix",
    )(*args)
    return outs[0], (v_first if has_vfirst else outs[1]), outs[-2], outs[-1]


def _merge_kernel(x_ref, sc_ref, sh_ref, gt_ref, g_ref, or_ref, ps0_ref, wi_ref, pw_ref, psc_ref, wr_ref,
                  wp_ref, wo_ref, o_ref, ps_ref, *, pos0):
    bt, tt, d = x_ref.shape
    m = bt * tt
    j = pl.program_id(1)

    @pl.when(j == 0)
    def _():
        ps_ref[...] = ps0_ref[...]

    ppg = _dot(_ada_norm(x_ref[...], g_ref[...], sc_ref[...], sh_ref[...]), wi_ref[...])
    pp = ppg[:, :POOL_WIDTH].reshape(bt, tt, POOL_WIDTH)
    ext3 = jnp.concatenate([ps_ref[...], pp], axis=1)
    ps_ref[...] = ext3[:, tt:tt + POOL_HALO, :]
    ext = ext3.reshape(bt * (POOL_HALO + tt), POOL_WIDTH)
    pos = pos0 + j * tt + lax.broadcasted_iota(jnp.int32, (bt, tt, POOL_GROUP), 1)
    acc, win, means = ext, 1, []
    for gi, target in enumerate(POOL_WINDOWS):
        while win < target:
            acc = acc + pltpu.roll(acc, win, 0)
            win *= 2
        tot = acc.reshape(bt, POOL_HALO + tt, POOL_WIDTH)[:, POOL_HALO:, gi * POOL_GROUP:(gi + 1) * POOL_GROUP]
        means.append(tot / jnp.minimum(pos + 1, target).astype(_F32))
    pooled = (jnp.concatenate(means, axis=-1) - pp).reshape(m, POOL_WIDTH).astype(_BF16)
    z = jnp.concatenate(
        [_dot(pooled[:, g * POOL_GROUP:(g + 1) * POOL_GROUP], pw_ref[g]) for g in range(N_POOL_GROUPS)],
        axis=-1) * psc_ref[...]
    a = _bdot(or_ref[...].reshape(m, MIX_WIDTH), wr_ref[...])
    b = _bdot(z, wp_ref[...])
    merged = (jax.nn.sigmoid(ppg[:, POOL_WIDTH:POOL_WIDTH + d]) * a
              + jax.nn.sigmoid(ppg[:, POOL_WIDTH + d:]) * b)
    o_ref[...] = x_ref[...] + gt_ref[...] * _bdot(merged, wo_ref[...]).reshape(bt, tt, d)


def _merge(x, sc, sh, gt, o_r, pool_prev, pos0, W, l):
    B, T, D = x.shape
    bt, tt = _tile(B, T)
    row = lambda n: pl.BlockSpec((bt, tt, n), lambda i, j: (i, j, 0))
    mod = pl.BlockSpec((bt, 1, D), lambda i, j: (i, 0, 0))
    halo = pl.BlockSpec((bt, POOL_HALO, POOL_WIDTH), lambda i, j: (i, 0, 0))
    names = ["w_in_pg", "pool_w", "pool_scale", "w_br_rwkv", "w_br_pool", "w_out"]
    ps0 = jnp.pad(pool_prev, ((0, 0), (POOL_HALO - POOL_BUF, 0), (0, 0)))
    x_new, pool_new = pl.pallas_call(
        functools.partial(_merge_kernel, pos0=pos0),
        grid=(B // bt, T // tt),
        in_specs=[row(D), mod, mod, mod, _layer(W["norm_mix"], l), row(MIX_WIDTH), halo]
        + [_layer(W[name], l) for name in names],
        out_specs=[row(D), halo],
        out_shape=[jax.ShapeDtypeStruct((B, T, D), _F32),
                   jax.ShapeDtypeStruct((B, POOL_HALO, POOL_WIDTH), _F32)],
        compiler_params=_cparams("arbitrary", "arbitrary"),
        name="merge",
    )(x, sc, sh, gt, W["norm_mix"], o_r, ps0, *[W[name] for name in names])
    return x_new, pool_new[:, POOL_HALO - POOL_BUF:]


def _mlp_kernel(x_ref, sc_ref, sh_ref, gt_ref, g_ref, w1_ref, w2_ref, gf_ref, o_ref, *, final_norm):
    bt, tt, d = x_ref.shape
    hb = _ada_norm(x_ref[...], g_ref[...], sc_ref[...], sh_ref[...])
    acc = jnp.zeros((bt * tt, d), _F32)
    for c in range(D_FF // FF_CHUNK):
        u = jnp.maximum(_dot(hb, w1_ref[:, c * FF_CHUNK:(c + 1) * FF_CHUNK]), 0.0)
        acc = acc + _bdot(u * u, w2_ref[c * FF_CHUNK:(c + 1) * FF_CHUNK, :])
    y = x_ref[...] + gt_ref[...] * acc.reshape(bt, tt, d)
    if final_norm:
        y = _rms(y, gf_ref[...])
    o_ref[...] = y


def _mlp(x, sc, sh, gt, W, l):
    B, T, D = x.shape
    bt, tt = _tile(B, T)
    row = pl.BlockSpec((bt, tt, D), lambda i, j: (i, j, 0))
    mod = pl.BlockSpec((bt, 1, D), lambda i, j: (i, 0, 0))
    return pl.pallas_call(
        functools.partial(_mlp_kernel, final_norm=(l == DEPTH - 1)),
        grid=(B // bt, T // tt),
        in_specs=[row, mod, mod, mod, _layer(W["norm_mlp"], l), _layer(W["w_ff1"], l), _layer(W["w_ff2"], l),
                  pl.BlockSpec((1, D), lambda i, j: (0, 0))],
        out_specs=row,
        out_shape=jax.ShapeDtypeStruct((B, T, D), _F32),
        compiler_params=_cparams("arbitrary", "arbitrary"),
        name="mlp",
    )(x, sc, sh, gt, W["norm_mlp"], W["w_ff1"], W["w_ff2"], W["norm_final"])


def _trunk(x, mods_mix, mods_mlp, shift0, pool0, wkv0, pos0, W, mix_tile):
    B = x.shape[0]
    shifts, pools, wkvs = [], [], []
    v_first = None
    for l in range(DEPTH):
        sh, sc, gt = (m.reshape(B, 1, D_MODEL) for m in jnp.split(mods_mix[l], 3, axis=-1))
        o_r, v_first, s_wkv, s_shift = _time_mix(x, sc, sh, shift0, v_first, wkv0, W, l, *mix_tile)
        x, s_pool = _merge(x, sc, sh, gt, o_r, pool0[l], pos0, W, l)
        sh, sc, gt = (m.reshape(B, 1, D_MODEL) for m in jnp.split(mods_mlp[l], 3, axis=-1))
        x = _mlp(x, sc, sh, gt, W, l)
        shifts.append(s_shift[:, 0])
        pools.append(s_pool)
        wkvs.append(s_wkv)
    return x, jnp.stack(shifts), jnp.stack(pools), jnp.stack(wkvs)


def kernel(x_prompt, x_sample, state_shift, state_pool, state_wkv, c_prompt, c_sample, w_ada_mix, b_ada_mix, norm_mix, w_in, mu_shift, w0, w2, a0, a2, g2, v0, v1, v2, k_k, k_a, r_k, ln_w, ln_b, pool_w, pool_scale, w_br_rwkv, w_br_pool, w_out, w_ada_mlp, b_ada_mlp, norm_mlp, w_ff1, w_ff2, norm_final):
    bf = lambda a: a.astype(_BF16)
    vec = lambda a: a.reshape(a.shape[0], 1, -1)
    heads = lambda a: a.reshape(a.shape[0], N_HEADS, 1, HEAD_SIZE)
    W = {
        "norm_mix": vec(norm_mix), "w_in_r": bf(w_in[:, :, :RWKV_COLS]), "w_in_pg": bf(w_in[:, :, RWKV_COLS:]),
        "mu_shift": vec(mu_shift), "w0": vec(w0), "w2": w2, "a0": vec(a0), "a2": a2, "g2": g2,
        "v0": vec(v0), "v1": v1, "v2": v2, "k_k": vec(k_k), "k_a": vec(k_a), "r_k": vec(r_k),
        "ln_w": heads(ln_w), "ln_b": heads(ln_b), "pool_w": bf(pool_w), "pool_scale": vec(pool_scale),
        "w_br_rwkv": bf(w_br_rwkv), "w_br_pool": bf(w_br_pool), "w_out": bf(w_out),
        "norm_mlp": vec(norm_mlp), "w_ff1": bf(w_ff1), "w_ff2": bf(w_ff2), "norm_final": norm_final.reshape(1, -1),
    }
    Bp = x_prompt.shape[0]
    c_all = jnp.concatenate([c_prompt, c_sample], axis=0)
    mods_mix = _ada_mod(c_all, w_ada_mix, b_ada_mix)
    mods_mlp = _ada_mod(c_all, w_ada_mlp, b_ada_mlp)
    shift0 = jnp.zeros((DEPTH, Bp, 1, RWKV_COLS), _F32)
    pool0 = jnp.zeros((DEPTH, Bp, POOL_BUF, POOL_WIDTH), _F32)
    wkv0 = jnp.zeros((DEPTH, Bp, N_HEADS, HEAD_SIZE, HEAD_SIZE), _F32)
    y_p, shift_p, pool_p, wkv_p = _trunk(x_prompt, mods_mix[:, :Bp], mods_mlp[:, :Bp], shift0, pool0, wkv0, 0, W,
                                         mix_tile=(PROMPT_SEQS_PER_TILE, WKV_CHUNK, PROMPT_SEQS_PER_GROUP))
    y_s, shift_s, pool_s, wkv_s = _trunk(x_sample, mods_mix[:, Bp:], mods_mlp[:, Bp:],
                                         state_shift[:, :, None, :], state_pool, state_wkv, PAST_LEN, W,
                                         mix_tile=(SAMPLE_SEQS_PER_TILE, x_sample.shape[1], SAMPLE_SEQS_PER_TILE))
    return (y_p, y_s, shift_p, pool_p, wkv_p, shift_s, pool_s, wkv_s)
```

```python
import functools

import jax
import jax.numpy as jnp
from jax import lax
from jax.experimental import pallas as pl
from jax.experimental.pallas import tpu as pltpu

D_MODEL = 1024
DEPTH = 4
PAST_LEN = 16384
MIX_WIDTH = D_MODEL // 2
HEAD_SIZE = 64
N_HEADS = MIX_WIDTH // HEAD_SIZE
D_DECAY_LORA = 64
D_AAA_LORA = 64
D_GATE_LORA = 128
POOL_WIDTH = D_MODEL // 2
POOL_WINDOWS = (2, 4, 8, 16)
N_POOL_GROUPS = len(POOL_WINDOWS)
POOL_GROUP = POOL_WIDTH // N_POOL_GROUPS
POOL_BUF = max(POOL_WINDOWS) - 1
POOL_HALO = max(POOL_WINDOWS)
D_FF = 4 * D_MODEL
RWKV_COLS = 3 * MIX_WIDTH + D_DECAY_LORA + D_AAA_LORA + D_GATE_LORA
GATE_COLS = 2 * D_MODEL
RWKV_SPLITS = (MIX_WIDTH, 2 * MIX_WIDTH, 3 * MIX_WIDTH,
               3 * MIX_WIDTH + D_DECAY_LORA, 3 * MIX_WIDTH + D_DECAY_LORA + D_AAA_LORA)
NORM_EPS = 1e-6
GN_EPS = 64e-5
LOG_DECAY_SCALE = -0.6065306597126334

V7X_VMEM_LIMIT_BYTES = 56 * 1024 * 1024
ROW_TILE = 512
FF_CHUNK = 1024
WKV_CHUNK = 64
WKV_SOLVE_BLOCK = 16
SAMPLE_SEQS_PER_TILE = 16
PROMPT_SEQS_PER_TILE = 4
PROMPT_SEQS_PER_GROUP = 4

_BF16 = jnp.bfloat16
_F32 = jnp.float32


def _cparams(*sem):
    return pltpu.CompilerParams(dimension_semantics=sem, vmem_limit_bytes=V7X_VMEM_LIMIT_BYTES)


def _dot(a, b):
    return jnp.dot(a, b, preferred_element_type=_F32)


def _bdot(a, b):
    return _dot(a.astype(_BF16), b.astype(_BF16))


def _bmm(a, b, contract=(2, 1)):
    dims = (((contract[0],), (contract[1],)), ((0,), (0,)))
    return lax.dot_general(a.astype(_BF16), b.astype(_BF16), dims, preferred_element_type=_F32)


def _split2(x):
    hi = x.astype(_BF16)
    return hi, (x - hi.astype(_F32)).astype(_BF16)


def _split3(x):
    hi, _ = _split2(x)
    r1 = x - hi.astype(_F32)
    mid, lo = _split2(r1)
    return hi, mid, lo


def _rms(x, g):
    return x * lax.rsqrt(jnp.mean(x * x, axis=-1, keepdims=True) + NORM_EPS) * g


def _ada_norm(x, g, sc, sh):
    bt, tt, d = x.shape
    h = _rms(x, g) * (1.0 + sc) + sh
    return h.reshape(bt * tt, d).astype(_BF16)


def _layer(a, l, single_buffer=False):
    mode = dict(pipeline_mode=pl.Buffered(1)) if single_buffer else {}
    return pl.BlockSpec((None,) + a.shape[1:], lambda i, j: (l,) + (0,) * (a.ndim - 1), **mode)


def _tile(B, T):
    tt = min(T, ROW_TILE)
    return ROW_TILE // tt, tt


def _ada_kernel(c_ref, w_ref, b_ref, o_ref):
    c = c_ref[...]
    o_ref[0] = _bdot(c * jax.nn.sigmoid(c), w_ref[0]) + b_ref[0]


def _ada_mod(c, w, b, tn=1024):
    L, D, N = w.shape
    M = c.shape[0]
    return pl.pallas_call(
        _ada_kernel,
        grid=(L, N // tn),
        in_specs=[
            pl.BlockSpec((M, D), lambda l, j: (0, 0)),
            pl.BlockSpec((1, D, tn), lambda l, j: (l, 0, j)),
            pl.BlockSpec((1, 1, tn), lambda l, j: (l, 0, j)),
        ],
        out_specs=pl.BlockSpec((1, M, tn), lambda l, j: (l, 0, j)),
        out_shape=jax.ShapeDtypeStruct((L, M, N), _F32),
        compiler_params=_cparams("arbitrary", "arbitrary"),
        name="ada_mod",
    )(c, w, b.reshape(L, 1, N))


def _wkv_chunk_math(kkp, rp, bn, kn, bnc, knc, v, g_last, s_prev):
    G, C, N = v.shape
    blk_size = min(WKV_SOLVE_BLOCK, C)
    ti = lax.broadcasted_iota(jnp.int32, (G, C, C), 1)
    si = lax.broadcasted_iota(jnp.int32, (G, C, C), 2)
    eye = (ti == si).astype(_F32)
    strict = ti > si
    blk = (ti // blk_size) == (si // blk_size)
    t2 = lax.broadcasted_iota(jnp.int32, (G, C, 2 * C), 1)
    s2 = lax.broadcasted_iota(jnp.int32, (G, C, 2 * C), 2)
    mask_kk = (s2 >= C) & (t2 > s2 - C)
    mask_y = ((s2 < C) & (t2 >= s2)) | ((s2 >= C) & (t2 >= s2 - C))

    lhs = jnp.concatenate([kkp, rp], axis=1)
    rhs = jnp.concatenate([bn, kn], axis=1)
    a = _bmm(lhs, rhs, (2, 2))
    w1 = _bmm(lhs, s_prev, (2, 2))
    rhs_u = w1[:, :C] + _bmm(jnp.where(mask_kk, a[:, :C], 0.0), jnp.concatenate([v, v], axis=1))
    a_kb = jnp.where(strict, a[:, :C, :C], 0.0)
    d = jnp.where(blk, a_kb, 0.0)
    tm = eye - d
    p = d
    n = 1
    while 2 * n < blk_size:
        p = _bmm(p, p)
        tm = _bmm(tm, eye + p)
        n *= 2
    x = _bmm(tm, rhs_u)
    if C > blk_size:
        mo = _bmm(tm, a_kb - d)
        x = x - _bmm(mo, x)
        p = mo
        n = 1
        while 2 * n < C // blk_size:
            p = _bmm(p, p)
            x = x + _bmm(p, x)
            n *= 2
    nuv = jnp.concatenate([-x, v], axis=1)
    y = w1[:, C:] + _bmm(jnp.where(mask_y, a[:, C:], 0.0), nuv)
    s_new = s_prev * g_last + _bmm(nuv, jnp.concatenate([bnc, knc], axis=1), (1, 1))
    return y, s_new


def _mix_kernel(*refs, has_vfirst, group):
    it = iter(refs)
    x_ref, sc_ref, sh_ref, g_ref, wi_ref, sh0_ref = (next(it) for _ in range(6))
    vf_ref = next(it) if has_vfirst else None
    s0_ref, ones_ref, mu_ref, w0_ref, w2_ref, a0_ref, a2_ref, g2_ref = (next(it) for _ in range(8))
    v0_ref, v1_ref, v2_ref = (next(it) for _ in range(3)) if has_vfirst else (None, None, None)
    kk_ref, ka_ref, rk_ref, lnw_ref, lnb_ref = (next(it) for _ in range(5))
    o_ref = next(it)
    vfo_ref = None if has_vfirst else next(it)
    s_ref, last_ref = next(it), next(it)

    @pl.when(pl.program_id(1) == 0)
    def _():
        last_ref[...] = sh0_ref[...]
        s_ref[...] = s0_ref[...]

    L = x_ref.shape[1]
    n_seq = group
    rows = n_seq * L
    H, N, M = N_HEADS, HEAD_SIZE, MIX_WIDTH

    def head_sum(t):
        return sum(_dot(p, ones_ref[...]) for p in _split2(t))

    def split_heads(t):
        t3 = t.reshape(n_seq, -1, M)
        return jnp.concatenate([t3[:, :, h * N:(h + 1) * N] for h in range(H)], axis=0)

    def per_chain(ref):
        return jnp.broadcast_to(ref[...][:, None], (H, n_seq, 1, N)).reshape(H * n_seq, 1, N)

    for lo in range(0, x_ref.shape[0], group):
        sl = slice(lo, lo + group)
        x = _dot(_ada_norm(x_ref[sl], g_ref[...], sc_ref[sl], sh_ref[sl]), wi_ref[...])
        init = jnp.broadcast_to(last_ref[sl], (n_seq, L, RWKV_COLS)).reshape(rows, RWKV_COLS)
        first = (lax.broadcasted_iota(jnp.int32, (rows, RWKV_COLS), 0) & (L - 1)) == 0
        prev = jnp.where(first, init, pltpu.roll(x, 1, 0))
        last_ref[sl] = x.reshape(n_seq, L, RWKV_COLS)[:, L - 1:L, :]
        xl = x + (prev - x) * mu_ref[...]
        r, k, v = xl[:, :M], xl[:, M:2 * M], xl[:, 2 * M:3 * M]
        xw, xa, xg = (xl[:, RWKV_SPLITS[2]:RWKV_SPLITS[3]], xl[:, RWKV_SPLITS[3]:RWKV_SPLITS[4]],
                      xl[:, RWKV_SPLITS[4]:])
        lw = LOG_DECAY_SCALE * jax.nn.sigmoid(w0_ref[...] + _bdot(jnp.tanh(xw), w2_ref[...]))
        a = jax.nn.sigmoid(a0_ref[...] + _bdot(xa, a2_ref[...]))
        g = _bdot(jax.nn.sigmoid(xg), g2_ref[...])
        if has_vfirst:
            vf = vf_ref[sl].reshape(rows, M)
            v = v + (vf - v) * jax.nn.sigmoid(v0_ref[...] + _bdot(_bdot(v, v1_ref[...]), v2_ref[...]))
        else:
            vfo_ref[sl] = v.reshape(n_seq, L, M)
        kk = k * kk_ref[...]
        kk = kk * lax.rsqrt(head_sum(kk * kk) + 1e-12)
        k = k * (1.0 + (a - 1.0) * ka_ref[...])
        b = kk * a
        lw3 = lw.reshape(n_seq, L, M)
        ti = lax.broadcasted_iota(jnp.int32, (n_seq, L, L), 1)
        si = lax.broadcasted_iota(jnp.int32, (n_seq, L, L), 2)
        tril_incl = (ti >= si).astype(_BF16)
        cl3 = sum(_bmm(tril_incl, p) for p in _split3(lw3))
        cl_last3 = cl3[:, L - 1:L, :]
        cl = cl3.reshape(rows, M)
        e_neg = jnp.exp(-cl)
        e_out = jnp.exp(cl_last3 - cl3).reshape(rows, M)
        operands = (kk * jnp.exp(cl - lw), r * jnp.exp(cl), b * e_neg, k * e_neg, b * e_out, k * e_out)
        vh = split_heads(v)
        y, s_new = _wkv_chunk_math(*(split_heads(t) for t in operands), vh,
                                   split_heads(jnp.exp(cl_last3).reshape(n_seq, M)),
                                   jnp.concatenate([s_ref[sl, h] for h in range(H)], axis=0))
        for h in range(H):
            s_ref[sl, h] = s_new[h * n_seq:(h + 1) * n_seq]
        yc = y - jnp.mean(y, axis=-1, keepdims=True)
        var = jnp.mean(yc * yc, axis=-1, keepdims=True)
        yn = yc * lax.rsqrt(var + GN_EPS) * per_chain(lnw_ref) + per_chain(lnb_ref)
        bonus = jnp.sum(split_heads(r * k * rk_ref[...]), axis=-1, keepdims=True) * vh
        o = (yn + bonus) * split_heads(g)
        o_ref[sl] = jnp.concatenate([o[h * n_seq:(h + 1) * n_seq] for h in range(H)], axis=-1)


def _time_mix(x, sc, sh, shift_prev, v_first, s0, W, l, n_seq, seq_len, group):
    B, T, D = x.shape
    H, N, M = N_HEADS, HEAD_SIZE, MIX_WIDTH
    has_vfirst = v_first is not None
    row = lambda n: pl.BlockSpec((n_seq, seq_len, n), lambda i, j: (i, j, 0))
    per_seq = lambda n: pl.BlockSpec((n_seq, 1, n), lambda i, j: (i, 0, 0))
    state = pl.BlockSpec((None, n_seq, H, N, N), lambda i, j: (l, i, 0, 0, 0))
    ones_bd = jnp.kron(jnp.eye(H, dtype=_F32), jnp.ones((N, N), _F32)).astype(_BF16)
    names = ["mu_shift", "w0", "w2", "a0", "a2", "g2"] + (["v0", "v1", "v2"] if has_vfirst else [])
    names += ["k_k", "k_a", "r_k", "ln_w", "ln_b"]
    layer_of = lambda name: l - 1 if name in ("v0", "v1", "v2") else l
    args = [x, sc, sh, W["norm_mix"], W["w_in_r"], shift_prev]
    in_specs = [row(D), per_seq(D), per_seq(D), _layer(W["norm_mix"], l), _layer(W["w_in_r"], l),
                pl.BlockSpec((None, n_seq, 1, RWKV_COLS), lambda i, j: (l, i, 0, 0))]
    if has_vfirst:
        args.append(v_first)
        in_specs.append(row(M))
    state_arg = len(args)
    args += [s0, ones_bd] + [W[name] for name in names]
    in_specs += [state, pl.BlockSpec(ones_bd.shape, lambda i, j: (0, 0))]
    in_specs += [_layer(W[name], layer_of(name)) for name in names]
    out_specs = [row(M)] + ([] if has_vfirst else [row(M)]) + [state, per_seq(RWKV_COLS)]
    out_shape = [jax.ShapeDtypeStruct((B, T, M), _F32)] * (1 if has_vfirst else 2)
    out_shape += [jax.ShapeDtypeStruct(s0.shape, _F32), jax.ShapeDtypeStruct((B, 1, RWKV_COLS), _F32)]
    outs = pl.pallas_call(
        functools.partial(_mix_kernel, has_vfirst=has_vfirst, group=group),
        grid=(B // n_seq, T // seq_len),
        in_specs=in_specs,
        out_specs=out_specs,
        out_shape=out_shape,
        input_output_aliases={state_arg: len(out_shape) - 2},
        compiler_params=_cparams("arbitrary", "arbitrary"),
        name="time_mix",
    )(*args)
    return outs[0], (v_first if has_vfirst else outs[1]), outs[-2], outs[-1]


def _merge_mlp_kernel(x_ref, sc_ref, sh_ref, gt_ref, g_ref, or_ref, ps0_ref, sc2_ref, sh2_ref, gt2_ref, g2_ref,
                      wi_ref, pw_ref, psc_ref, wr_ref, wp_ref, wo_ref, w1_ref, w2_ref, gf_ref,
                      o_ref, ps_ref, *, pos0, final_norm):
    bt, tt, d = x_ref.shape
    m = bt * tt
    j = pl.program_id(1)

    @pl.when(j == 0)
    def _():
        ps_ref[...] = ps0_ref[...]

    ppg = _dot(_ada_norm(x_ref[...], g_ref[...], sc_ref[...], sh_ref[...]), wi_ref[...])
    pp = ppg[:, :POOL_WIDTH].reshape(bt, tt, POOL_WIDTH)
    ext3 = jnp.concatenate([ps_ref[...], pp], axis=1)
    ps_ref[...] = ext3[:, tt:tt + POOL_HALO, :]
    ext = ext3.reshape(bt * (POOL_HALO + tt), POOL_WIDTH)
    pos = pos0 + j * tt + lax.broadcasted_iota(jnp.int32, (bt, tt, POOL_GROUP), 1)
    acc, win, means = ext, 1, []
    for gi, target in enumerate(POOL_WINDOWS):
        while win < target:
            acc = acc + pltpu.roll(acc, win, 0)
            win *= 2
        tot = acc.reshape(bt, POOL_HALO + tt, POOL_WIDTH)[:, POOL_HALO:, gi * POOL_GROUP:(gi + 1) * POOL_GROUP]
        means.append(tot / jnp.minimum(pos + 1, target).astype(_F32))
    pooled = (jnp.concatenate(means, axis=-1) - pp).reshape(m, POOL_WIDTH).astype(_BF16)
    z = jnp.concatenate(
        [_dot(pooled[:, g * POOL_GROUP:(g + 1) * POOL_GROUP], pw_ref[g]) for g in range(N_POOL_GROUPS)],
        axis=-1) * psc_ref[...]
    a = _bdot(or_ref[...].reshape(m, MIX_WIDTH), wr_ref[...])
    b = _bdot(z, wp_ref[...])
    merged = (jax.nn.sigmoid(ppg[:, POOL_WIDTH:POOL_WIDTH + d]) * a
              + jax.nn.sigmoid(ppg[:, POOL_WIDTH + d:]) * b)
    x1 = x_ref[...] + gt_ref[...] * _bdot(merged, wo_ref[...]).reshape(bt, tt, d)
    hb = _ada_norm(x1, g2_ref[...], sc2_ref[...], sh2_ref[...])
    acc = jnp.zeros((m, d), _F32)
    for c in range(D_FF // FF_CHUNK):
        u = jnp.maximum(_dot(hb, w1_ref[:, c * FF_CHUNK:(c + 1) * FF_CHUNK]), 0.0)
        acc = acc + _bdot(u * u, w2_ref[c * FF_CHUNK:(c + 1) * FF_CHUNK, :])
    y = x1 + gt2_ref[...] * acc.reshape(bt, tt, d)
    if final_norm:
        y = _rms(y, gf_ref[...])
    o_ref[...] = y


def _merge_mlp(x, mods_mix, mods_mlp, o_r, pool_prev, pos0, W, l):
    B, T, D = x.shape
    bt, tt = _tile(B, T)
    row = lambda n: pl.BlockSpec((bt, tt, n), lambda i, j: (i, j, 0))
    mod = pl.BlockSpec((bt, 1, D), lambda i, j: (i, 0, 0))
    halo = pl.BlockSpec((bt, POOL_HALO, POOL_WIDTH), lambda i, j: (i, 0, 0))
    names = ["w_in_pg", "pool_w", "pool_scale", "w_br_rwkv", "w_br_pool", "w_out", "w_ff1", "w_ff2"]
    ps0 = jnp.pad(pool_prev, ((0, 0), (POOL_HALO - POOL_BUF, 0), (0, 0)))
    (sh, sc, gt), (sh2, sc2, gt2) = mods_mix, mods_mlp
    x_new, pool_new = pl.pallas_call(
        functools.partial(_merge_mlp_kernel, pos0=pos0, final_norm=(l == DEPTH - 1)),
        grid=(B // bt, T // tt),
        in_specs=[row(D), mod, mod, mod, _layer(W["norm_mix"], l), row(MIX_WIDTH), halo,
                  mod, mod, mod, _layer(W["norm_mlp"], l)]
        + [_layer(W[name], l, single_buffer=True) for name in names]
        + [pl.BlockSpec((1, D), lambda i, j: (0, 0))],
        out_specs=[row(D), halo],
        out_shape=[jax.ShapeDtypeStruct((B, T, D), _F32),
                   jax.ShapeDtypeStruct((B, POOL_HALO, POOL_WIDTH), _F32)],
        compiler_params=_cparams("arbitrary", "arbitrary"),
        name="merge_mlp",
    )(x, sc, sh, gt, W["norm_mix"], o_r, ps0, sc2, sh2, gt2, W["norm_mlp"],
      *[W[name] for name in names], W["norm_final"])
    return x_new, pool_new[:, POOL_HALO - POOL_BUF:]


def _trunk(x, mods_mix, mods_mlp, shift0, pool0, wkv0, pos0, W, mix_tile):
    B = x.shape[0]
    shifts, pools, wkv = [], [], wkv0
    v_first = None
    for l in range(DEPTH):
        mix = [m.reshape(B, 1, D_MODEL) for m in jnp.split(mods_mix[l], 3, axis=-1)]
        mlp = [m.reshape(B, 1, D_MODEL) for m in jnp.split(mods_mlp[l], 3, axis=-1)]
        o_r, v_first, wkv, s_shift = _time_mix(x, mix[1], mix[0], shift0, v_first, wkv, W, l, *mix_tile)
        x, s_pool = _merge_mlp(x, mix, mlp, o_r, pool0[l], pos0, W, l)
        shifts.append(s_shift[:, 0])
        pools.append(s_pool)
    return x, jnp.stack(shifts), jnp.stack(pools), wkv


def kernel(x_prompt, x_sample, state_shift, state_pool, state_wkv, c_prompt, c_sample, w_ada_mix, b_ada_mix, norm_mix, w_in, mu_shift, w0, w2, a0, a2, g2, v0, v1, v2, k_k, k_a, r_k, ln_w, ln_b, pool_w, pool_scale, w_br_rwkv, w_br_pool, w_out, w_ada_mlp, b_ada_mlp, norm_mlp, w_ff1, w_ff2, norm_final):
    bf = lambda a: a.astype(_BF16)
    vec = lambda a: a.reshape(a.shape[0], 1, -1)
    heads = lambda a: a.reshape(a.shape[0], N_HEADS, 1, HEAD_SIZE)
    W = {
        "norm_mix": vec(norm_mix), "w_in_r": bf(w_in[:, :, :RWKV_COLS]), "w_in_pg": bf(w_in[:, :, RWKV_COLS:]),
        "mu_shift": vec(mu_shift), "w0": vec(w0), "w2": w2, "a0": vec(a0), "a2": a2, "g2": g2,
        "v0": vec(v0), "v1": v1, "v2": v2, "k_k": vec(k_k), "k_a": vec(k_a), "r_k": vec(r_k),
        "ln_w": heads(ln_w), "ln_b": heads(ln_b), "pool_w": bf(pool_w), "pool_scale": vec(pool_scale),
        "w_br_rwkv": bf(w_br_rwkv), "w_br_pool": bf(w_br_pool), "w_out": bf(w_out),
        "norm_mlp": vec(norm_mlp), "w_ff1": bf(w_ff1), "w_ff2": bf(w_ff2), "norm_final": norm_final.reshape(1, -1),
    }
    Bp = x_prompt.shape[0]
    c_all = jnp.concatenate([c_prompt, c_sample], axis=0)
    mods_mix = _ada_mod(c_all, w_ada_mix, b_ada_mix)
    mods_mlp = _ada_mod(c_all, w_ada_mlp, b_ada_mlp)
    shift0 = jnp.zeros((DEPTH, Bp, 1, RWKV_COLS), _F32)
    pool0 = jnp.zeros((DEPTH, Bp, POOL_BUF, POOL_WIDTH), _F32)
    wkv0 = jnp.zeros((DEPTH, Bp, N_HEADS, HEAD_SIZE, HEAD_SIZE), _F32)
    y_p, shift_p, pool_p, wkv_p = _trunk(x_prompt, mods_mix[:, :Bp], mods_mlp[:, :Bp], shift0, pool0, wkv0, 0, W,
                                         mix_tile=(PROMPT_SEQS_PER_TILE, WKV_CHUNK, PROMPT_SEQS_PER_GROUP))
    y_s, shift_s, pool_s, wkv_s = _trunk(x_sample, mods_mix[:, Bp:], mods_mlp[:, Bp:],
                                         state_shift[:, :, None, :], state_pool, state_wkv, PAST_LEN, W,
                                         mix_tile=(SAMPLE_SEQS_PER_TILE, x_sample.shape[1], SAMPLE_SEQS_PER_TILE))
    return (y_p, y_s, shift_p, pool_p, wkv_p, shift_s, pool_s, wkv_s)
```

```python
import functools
import itertools

import jax
import jax.numpy as jnp
from jax import lax
from jax.experimental import pallas as pl
from jax.experimental.pallas import tpu as pltpu

D_MODEL = 1024
DEPTH = 4
PAST_LEN = 16384
MIX_WIDTH = D_MODEL // 2
HEAD_SIZE = 64
N_HEADS = MIX_WIDTH // HEAD_SIZE
D_DECAY_LORA = 64
D_AAA_LORA = 64
D_GATE_LORA = 128
POOL_WIDTH = D_MODEL // 2
POOL_WINDOWS = (2, 4, 8, 16)
N_POOL_GROUPS = len(POOL_WINDOWS)
POOL_GROUP = POOL_WIDTH // N_POOL_GROUPS
POOL_BUF = max(POOL_WINDOWS) - 1
POOL_HALO = max(POOL_WINDOWS)
D_FF = 4 * D_MODEL
RWKV_COLS = 3 * MIX_WIDTH + D_DECAY_LORA + D_AAA_LORA + D_GATE_LORA
GATE_COLS = 2 * D_MODEL
RWKV_SPLITS = (MIX_WIDTH, 2 * MIX_WIDTH, 3 * MIX_WIDTH,
               3 * MIX_WIDTH + D_DECAY_LORA, 3 * MIX_WIDTH + D_DECAY_LORA + D_AAA_LORA)
NORM_EPS = 1e-6
GN_EPS = 64e-5
LOG_DECAY_SCALE = -0.6065306597126334

V7X_VMEM_LIMIT_BYTES = 56 * 1024 * 1024
ROW_TILE = 512
FF_CHUNK = 1024
WKV_CHUNK = 64
WKV_SOLVE_BLOCK = 16
SAMPLE_SEQS_PER_TILE = 16
PROMPT_SEQS_PER_TILE = 4
PROMPT_SEQS_PER_GROUP = 4
PROMPT_CHUNKS_PER_STEP = 4

_BF16 = jnp.bfloat16
_F32 = jnp.float32


def _cparams(*sem):
    return pltpu.CompilerParams(dimension_semantics=sem, vmem_limit_bytes=V7X_VMEM_LIMIT_BYTES)


def _dot(a, b):
    return jnp.dot(a, b, preferred_element_type=_F32)


def _bdot(a, b):
    return _dot(a.astype(_BF16), b.astype(_BF16))


def _bmm(a, b, contract=(2, 1)):
    dims = (((contract[0],), (contract[1],)), ((0,), (0,)))
    return lax.dot_general(a.astype(_BF16), b.astype(_BF16), dims, preferred_element_type=_F32)


def _split2(x):
    hi = x.astype(_BF16)
    return hi, (x - hi.astype(_F32)).astype(_BF16)


def _split3(x):
    hi, _ = _split2(x)
    r1 = x - hi.astype(_F32)
    mid, lo = _split2(r1)
    return hi, mid, lo


def _rms(x, g):
    return x * lax.rsqrt(jnp.mean(x * x, axis=-1, keepdims=True) + NORM_EPS) * g


def _ada_norm(x, g, sc, sh):
    bt, tt, d = x.shape
    h = _rms(x, g) * (1.0 + sc) + sh
    return h.reshape(bt * tt, d).astype(_BF16)


def _layer(a, l, single_buffer=False):
    mode = dict(pipeline_mode=pl.Buffered(1)) if single_buffer else {}
    return pl.BlockSpec((None,) + a.shape[1:], lambda i, j: (l,) + (0,) * (a.ndim - 1), **mode)


def _tile(B, T):
    tt = min(T, ROW_TILE)
    return ROW_TILE // tt, tt


def _ada_kernel(c_ref, w_ref, b_ref, o_ref):
    c = c_ref[...]
    o_ref[0] = _bdot(c * jax.nn.sigmoid(c), w_ref[0]) + b_ref[0]


def _ada_mod(c, w, b, tn=1024):
    L, D, N = w.shape
    M = c.shape[0]
    return pl.pallas_call(
        _ada_kernel,
        grid=(L, N // tn),
        in_specs=[
            pl.BlockSpec((M, D), lambda l, j: (0, 0)),
            pl.BlockSpec((1, D, tn), lambda l, j: (l, 0, j)),
            pl.BlockSpec((1, 1, tn), lambda l, j: (l, 0, j)),
        ],
        out_specs=pl.BlockSpec((1, M, tn), lambda l, j: (l, 0, j)),
        out_shape=jax.ShapeDtypeStruct((L, M, N), _F32),
        compiler_params=_cparams("arbitrary", "arbitrary"),
        name="ada_mod",
    )(c, w, b.reshape(L, 1, N))


def _wkv_chunk_math(kkp, rp, bn, kn, bnc, knc, v, g_last, s_prev):
    G, C, N = v.shape
    blk_size = min(WKV_SOLVE_BLOCK, C)
    ti = lax.broadcasted_iota(jnp.int32, (G, C, C), 1)
    si = lax.broadcasted_iota(jnp.int32, (G, C, C), 2)
    eye = (ti == si).astype(_F32)
    strict = ti > si
    blk = (ti // blk_size) == (si // blk_size)
    t2 = lax.broadcasted_iota(jnp.int32, (G, C, 2 * C), 1)
    s2 = lax.broadcasted_iota(jnp.int32, (G, C, 2 * C), 2)
    mask_kk = (s2 >= C) & (t2 > s2 - C)
    mask_y = ((s2 < C) & (t2 >= s2)) | ((s2 >= C) & (t2 >= s2 - C))

    lhs = jnp.concatenate([kkp, rp], axis=1)
    rhs = jnp.concatenate([bn, kn], axis=1)
    a = _bmm(lhs, rhs, (2, 2))
    w1 = _bmm(lhs, s_prev, (2, 2))
    rhs_u = w1[:, :C] + _bmm(jnp.where(mask_kk, a[:, :C], 0.0), jnp.concatenate([v, v], axis=1))
    a_kb = jnp.where(strict, a[:, :C, :C], 0.0)
    d = jnp.where(blk, a_kb, 0.0)
    tm = eye - d
    p = d
    n = 1
    while 2 * n < blk_size:
        p = _bmm(p, p)
        tm = _bmm(tm, eye + p)
        n *= 2
    x = _bmm(tm, rhs_u)
    if C > blk_size:
        mo = _bmm(tm, a_kb - d)
        x = x - _bmm(mo, x)
        p = mo
        n = 1
        while 2 * n < C // blk_size:
            p = _bmm(p, p)
            x = x + _bmm(p, x)
            n *= 2
    nuv = jnp.concatenate([-x, v], axis=1)
    y = w1[:, C:] + _bmm(jnp.where(mask_y, a[:, C:], 0.0), nuv)
    s_new = s_prev * g_last + _bmm(nuv, jnp.concatenate([bnc, knc], axis=1), (1, 1))
    return y, s_new


def _mix_kernel(*refs, has_vfirst, group, chunk):
    it = iter(refs)
    x_ref, sc_ref, sh_ref, g_ref, wi_ref, sh0_ref = (next(it) for _ in range(6))
    vf_ref = next(it) if has_vfirst else None
    s0_ref, ones_ref, mu_ref, w0_ref, w2_ref, a0_ref, a2_ref, g2_ref = (next(it) for _ in range(8))
    v0_ref, v1_ref, v2_ref = (next(it) for _ in range(3)) if has_vfirst else (None, None, None)
    kk_ref, ka_ref, rk_ref, lnw_ref, lnb_ref = (next(it) for _ in range(5))
    o_ref = next(it)
    vfo_ref = None if has_vfirst else next(it)
    s_ref, last_ref = next(it), next(it)

    @pl.when(pl.program_id(1) == 0)
    def _():
        last_ref[...] = sh0_ref[...]
        s_ref[...] = s0_ref[...]

    L = chunk
    n_seq = group
    rows = n_seq * L
    H, N, M = N_HEADS, HEAD_SIZE, MIX_WIDTH

    def head_sum(t):
        return sum(_dot(p, ones_ref[...]) for p in _split2(t))

    def split_heads(t):
        t3 = t.reshape(n_seq, -1, M)
        return jnp.concatenate([t3[:, :, h * N:(h + 1) * N] for h in range(H)], axis=0)

    def per_chain(ref):
        return jnp.broadcast_to(ref[...][:, None], (H, n_seq, 1, N)).reshape(H * n_seq, 1, N)

    for lo, t0 in itertools.product(range(0, x_ref.shape[0], group), range(0, x_ref.shape[1], L)):
        sl, tok = slice(lo, lo + group), slice(t0, t0 + L)
        x = _dot(_ada_norm(x_ref[sl, tok], g_ref[...], sc_ref[sl], sh_ref[sl]), wi_ref[...])
        init = jnp.broadcast_to(last_ref[sl], (n_seq, L, RWKV_COLS)).reshape(rows, RWKV_COLS)
        first = (lax.broadcasted_iota(jnp.int32, (rows, RWKV_COLS), 0) & (L - 1)) == 0
        prev = jnp.where(first, init, pltpu.roll(x, 1, 0))
        last_ref[sl] = x.reshape(n_seq, L, RWKV_COLS)[:, L - 1:L, :]
        xl = x + (prev - x) * mu_ref[...]
        r, k, v = xl[:, :M], xl[:, M:2 * M], xl[:, 2 * M:3 * M]
        xw, xa, xg = (xl[:, RWKV_SPLITS[2]:RWKV_SPLITS[3]], xl[:, RWKV_SPLITS[3]:RWKV_SPLITS[4]],
                      xl[:, RWKV_SPLITS[4]:])
        lw = LOG_DECAY_SCALE * jax.nn.sigmoid(w0_ref[...] + _bdot(jnp.tanh(xw), w2_ref[...]))
        a = jax.nn.sigmoid(a0_ref[...] + _bdot(xa, a2_ref[...]))
        g = _bdot(jax.nn.sigmoid(xg), g2_ref[...])
        if has_vfirst:
            vf = vf_ref[sl, tok].reshape(rows, M)
            v = v + (vf - v) * jax.nn.sigmoid(v0_ref[...] + _bdot(_bdot(v, v1_ref[...]), v2_ref[...]))
        else:
            vfo_ref[sl, tok] = v.reshape(n_seq, L, M)
        kk = k * kk_ref[...]
        kk = kk * lax.rsqrt(head_sum(kk * kk) + 1e-12)
        k = k * (1.0 + (a - 1.0) * ka_ref[...])
        b = kk * a
        lw3 = lw.reshape(n_seq, L, M)
        ti = lax.broadcasted_iota(jnp.int32, (n_seq, L, L), 1)
        si = lax.broadcasted_iota(jnp.int32, (n_seq, L, L), 2)
        tril_incl = (ti >= si).astype(_BF16)
        cl3 = sum(_bmm(tril_incl, p) for p in _split3(lw3))
        cl_last3 = cl3[:, L - 1:L, :]
        cl = cl3.reshape(rows, M)
        e_neg = jnp.exp(-cl)
        e_out = jnp.exp(cl_last3 - cl3).reshape(rows, M)
        operands = (kk * jnp.exp(cl - lw), r * jnp.exp(cl), b * e_neg, k * e_neg, b * e_out, k * e_out)
        vh = split_heads(v)
        y, s_new = _wkv_chunk_math(*(split_heads(t.astype(_BF16)) for t in operands), vh,
                                   split_heads(jnp.exp(cl_last3).reshape(n_seq, M)),
                                   jnp.concatenate([s_ref[sl, h] for h in range(H)], axis=0))
        for h in range(H):
            s_ref[sl, h] = s_new[h * n_seq:(h + 1) * n_seq]
        yc = y - jnp.mean(y, axis=-1, keepdims=True)
        var = jnp.mean(yc * yc, axis=-1, keepdims=True)
        yn = yc * lax.rsqrt(var + GN_EPS) * per_chain(lnw_ref) + per_chain(lnb_ref)
        bonus = jnp.sum(split_heads(r * k * rk_ref[...]), axis=-1, keepdims=True) * vh
        o = (yn + bonus) * split_heads(g)
        o_ref[sl, tok] = jnp.concatenate([o[h * n_seq:(h + 1) * n_seq] for h in range(H)], axis=-1)


def _time_mix(x, sc, sh, shift_prev, v_first, s0, W, l, n_seq, seq_len, group):
    B, T, D = x.shape
    H, N, M = N_HEADS, HEAD_SIZE, MIX_WIDTH
    has_vfirst = v_first is not None
    row = lambda n: pl.BlockSpec((n_seq, seq_len, n), lambda i, j: (i, j, 0))
    per_seq = lambda n: pl.BlockSpec((n_seq, 1, n), lambda i, j: (i, 0, 0))
    state = pl.BlockSpec((None, n_seq, H, N, N), lambda i, j: (l, i, 0, 0, 0))
    ones_bd = jnp.kron(jnp.eye(H, dtype=_F32), jnp.ones((N, N), _F32)).astype(_BF16)
    names = ["mu_shift", "w0", "w2", "a0", "a2", "g2"] + (["v0", "v1", "v2"] if has_vfirst else [])
    names += ["k_k", "k_a", "r_k", "ln_w", "ln_b"]
    layer_of = lambda name: l - 1 if name in ("v0", "v1", "v2") else l
    args = [x, sc, sh, W["norm_mix"], W["w_in_r"], shift_prev]
    in_specs = [row(D), per_seq(D), per_seq(D), _layer(W["norm_mix"], l), _layer(W["w_in_r"], l),
                pl.BlockSpec((None, n_seq, 1, RWKV_COLS), lambda i, j: (l, i, 0, 0))]
    if has_vfirst:
        args.append(v_first)
        in_specs.append(row(M))
    state_arg = len(args)
    args += [s0, ones_bd] + [W[name] for name in names]
    in_specs += [state, pl.BlockSpec(ones_bd.shape, lambda i, j: (0, 0))]
    in_specs += [_layer(W[name], layer_of(name)) for name in names]
    out_specs = [row(M)] + ([] if has_vfirst else [row(M)]) + [state, per_seq(RWKV_COLS)]
    out_shape = [jax.ShapeDtypeStruct((B, T, M), _F32)] * (1 if has_vfirst else 2)
    out_shape += [jax.ShapeDtypeStruct(s0.shape, _F32), jax.ShapeDtypeStruct((B, 1, RWKV_COLS), _F32)]
    outs = pl.pallas_call(
        functools.partial(_mix_kernel, has_vfirst=has_vfirst, group=group, chunk=min(seq_len, WKV_CHUNK)),
        grid=(B // n_seq, T // seq_len),
        in_specs=in_specs,
        out_specs=out_specs,
        out_shape=out_shape,
        input_output_aliases={state_arg: len(out_shape) - 2},
        compiler_params=_cparams("arbitrary", "arbitrary"),
        name="time_mix",
    )(*args)
    return outs[0], (v_first if has_vfirst else outs[1]), outs[-2], outs[-1]


def _merge_mlp_kernel(x_ref, sc_ref, sh_ref, gt_ref, g_ref, or_ref, ps0_ref, sc2_ref, sh2_ref, gt2_ref, g2_ref,
                      wi_ref, pw_ref, psc_ref, wr_ref, wp_ref, wo_ref, w1_ref, w2_ref, gf_ref,
                      o_ref, ps_ref, *, pos0, final_norm):
    bt, tt, d = x_ref.shape
    m = bt * tt
    j = pl.program_id(1)

    @pl.when(j == 0)
    def _():
        ps_ref[...] = ps0_ref[...]

    ppg = _dot(_ada_norm(x_ref[...], g_ref[...], sc_ref[...], sh_ref[...]), wi_ref[...])
    pp = ppg[:, :POOL_WIDTH].reshape(bt, tt, POOL_WIDTH)
    ext3 = jnp.concatenate([ps_ref[...], pp], axis=1)
    ps_ref[...] = ext3[:, tt:tt + POOL_HALO, :]
    ext = ext3.reshape(bt * (POOL_HALO + tt), POOL_WIDTH)
    pos = pos0 + j * tt + lax.broadcasted_iota(jnp.int32, (bt, tt, POOL_GROUP), 1)
    acc, win, means = ext, 1, []
    for gi, target in enumerate(POOL_WINDOWS):
        while win < target:
            acc = acc + pltpu.roll(acc, win, 0)
            win *= 2
        tot = acc.reshape(bt, POOL_HALO + tt, POOL_WIDTH)[:, POOL_HALO:, gi * POOL_GROUP:(gi + 1) * POOL_GROUP]
        means.append(tot / jnp.minimum(pos + 1, target).astype(_F32))
    pooled = (jnp.concatenate(means, axis=-1) - pp).reshape(m, POOL_WIDTH).astype(_BF16)
    z = jnp.concatenate(
        [_dot(pooled[:, g * POOL_GROUP:(g + 1) * POOL_GROUP], pw_ref[g]) for g in range(N_POOL_GROUPS)],
        axis=-1) * psc_ref[...]
    a = _bdot(or_ref[...].reshape(m, MIX_WIDTH), wr_ref[...])
    b = _bdot(z, wp_ref[...])
    merged = (jax.nn.sigmoid(ppg[:, POOL_WIDTH:POOL_WIDTH + d]) * a
              + jax.nn.sigmoid(ppg[:, POOL_WIDTH + d:]) * b)
    x1 = x_ref[...] + gt_ref[...] * _bdot(merged, wo_ref[...]).reshape(bt, tt, d)
    hb = _ada_norm(x1, g2_ref[...], sc2_ref[...], sh2_ref[...])
    acc = jnp.zeros((m, d), _F32)
    for c in range(D_FF // FF_CHUNK):
        u = jnp.maximum(_dot(hb, w1_ref[:, c * FF_CHUNK:(c + 1) * FF_CHUNK]), 0.0)
        acc = acc + _bdot(u * u, w2_ref[c * FF_CHUNK:(c + 1) * FF_CHUNK, :])
    y = x1 + gt2_ref[...] * acc.reshape(bt, tt, d)
    if final_norm:
        y = _rms(y, gf_ref[...])
    o_ref[...] = y


def _merge_mlp(x, mods_mix, mods_mlp, o_r, pool_prev, pos0, W, l):
    B, T, D = x.shape
    bt, tt = _tile(B, T)
    row = lambda n: pl.BlockSpec((bt, tt, n), lambda i, j: (i, j, 0))
    mod = pl.BlockSpec((bt, 1, D), lambda i, j: (i, 0, 0))
    halo = pl.BlockSpec((bt, POOL_HALO, POOL_WIDTH), lambda i, j: (i, 0, 0))
    names = ["w_in_pg", "pool_w", "pool_scale", "w_br_rwkv", "w_br_pool", "w_out", "w_ff1", "w_ff2"]
    ps0 = jnp.pad(pool_prev, ((0, 0), (POOL_HALO - POOL_BUF, 0), (0, 0)))
    (sh, sc, gt), (sh2, sc2, gt2) = mods_mix, mods_mlp
    x_new, pool_new = pl.pallas_call(
        functools.partial(_merge_mlp_kernel, pos0=pos0, final_norm=(l == DEPTH - 1)),
        grid=(B // bt, T // tt),
        in_specs=[row(D), mod, mod, mod, _layer(W["norm_mix"], l), row(MIX_WIDTH), halo,
                  mod, mod, mod, _layer(W["norm_mlp"], l)]
        + [_layer(W[name], l, single_buffer=True) for name in names]
        + [pl.BlockSpec((1, D), lambda i, j: (0, 0))],
        out_specs=[row(D), halo],
        out_shape=[jax.ShapeDtypeStruct((B, T, D), _F32),
                   jax.ShapeDtypeStruct((B, POOL_HALO, POOL_WIDTH), _F32)],
        compiler_params=_cparams("arbitrary", "arbitrary"),
        name="merge_mlp",
    )(x, sc, sh, gt, W["norm_mix"], o_r, ps0, sc2, sh2, gt2, W["norm_mlp"],
      *[W[name] for name in names], W["norm_final"])
    return x_new, pool_new[:, POOL_HALO - POOL_BUF:]


def _trunk(x, mods_mix, mods_mlp, shift0, pool0, wkv0, pos0, W, mix_tile):
    B = x.shape[0]
    shifts, pools, wkv = [], [], wkv0
    v_first = None
    for l in range(DEPTH):
        mix = [m.reshape(B, 1, D_MODEL) for m in jnp.split(mods_mix[l], 3, axis=-1)]
        mlp = [m.reshape(B, 1, D_MODEL) for m in jnp.split(mods_mlp[l], 3, axis=-1)]
        o_r, v_first, wkv, s_shift = _time_mix(x, mix[1], mix[0], shift0, v_first, wkv, W, l, *mix_tile)
        x, s_pool = _merge_mlp(x, mix, mlp, o_r, pool0[l], pos0, W, l)
        shifts.append(s_shift[:, 0])
        pools.append(s_pool)
    return x, jnp.stack(shifts), jnp.stack(pools), wkv


def kernel(x_prompt, x_sample, state_shift, state_pool, state_wkv, c_prompt, c_sample, w_ada_mix, b_ada_mix, norm_mix, w_in, mu_shift, w0, w2, a0, a2, g2, v0, v1, v2, k_k, k_a, r_k, ln_w, ln_b, pool_w, pool_scale, w_br_rwkv, w_br_pool, w_out, w_ada_mlp, b_ada_mlp, norm_mlp, w_ff1, w_ff2, norm_final):
    bf = lambda a: a.astype(_BF16)
    vec = lambda a: a.reshape(a.shape[0], 1, -1)
    heads = lambda a: a.reshape(a.shape[0], N_HEADS, 1, HEAD_SIZE)
    W = {
        "norm_mix": vec(norm_mix), "w_in_r": bf(w_in[:, :, :RWKV_COLS]), "w_in_pg": bf(w_in[:, :, RWKV_COLS:]),
        "mu_shift": vec(mu_shift), "w0": vec(w0), "w2": w2, "a0": vec(a0), "a2": a2, "g2": g2,
        "v0": vec(v0), "v1": v1, "v2": v2, "k_k": vec(k_k), "k_a": vec(k_a), "r_k": vec(r_k),
        "ln_w": heads(ln_w), "ln_b": heads(ln_b), "pool_w": bf(pool_w), "pool_scale": vec(pool_scale),
        "w_br_rwkv": bf(w_br_rwkv), "w_br_pool": bf(w_br_pool), "w_out": bf(w_out),
        "norm_mlp": vec(norm_mlp), "w_ff1": bf(w_ff1), "w_ff2": bf(w_ff2), "norm_final": norm_final.reshape(1, -1),
    }
    Bp = x_prompt.shape[0]
    c_all = jnp.concatenate([c_prompt, c_sample], axis=0)
    mods_mix = _ada_mod(c_all, w_ada_mix, b_ada_mix)
    mods_mlp = _ada_mod(c_all, w_ada_mlp, b_ada_mlp)
    shift0 = jnp.zeros((DEPTH, Bp, 1, RWKV_COLS), _F32)
    pool0 = jnp.zeros((DEPTH, Bp, POOL_BUF, POOL_WIDTH), _F32)
    wkv0 = jnp.zeros((DEPTH, Bp, N_HEADS, HEAD_SIZE, HEAD_SIZE), _F32)
    y_p, shift_p, pool_p, wkv_p = _trunk(x_prompt, mods_mix[:, :Bp], mods_mlp[:, :Bp], shift0, pool0, wkv0, 0, W,
                                         mix_tile=(PROMPT_SEQS_PER_TILE, PROMPT_CHUNKS_PER_STEP * WKV_CHUNK,
                                                   PROMPT_SEQS_PER_GROUP))
    y_s, shift_s, pool_s, wkv_s = _trunk(x_sample, mods_mix[:, Bp:], mods_mlp[:, Bp:],
                                         state_shift[:, :, None, :], state_pool, state_wkv, PAST_LEN, W,
                                         mix_tile=(SAMPLE_SEQS_PER_TILE, x_sample.shape[1], SAMPLE_SEQS_PER_TILE))
    return (y_p, y_s, shift_p, pool_p, wkv_p, shift_s, pool_s, wkv_s)
```
